```python
import jax, jax.numpy as jnp
from jax import lax
import numpy as np

D_MODEL = 1024
BATCH = 16
SEQ = 4096
DEPTH = 2

D_CONV = D_MODEL
CONV_WIDTH = 31
D_SGU = D_MODEL
SGU_GROUPS = 8
SGU_GROUP_DIM = D_SGU // SGU_GROUPS
CHUNK = 128
D_FF = ((8 * D_MODEL // 3 + 255) // 256) * 256
D_IN = 2 * D_CONV + 2 * D_SGU + 2 * D_MODEL
EPS = 1e-6

kernel_name = "hybrid_conformer_conv_gmlp_encoder"


def rmsnorm(x, g):
    xf = x.astype(jnp.float32)
    y = xf * lax.rsqrt(jnp.mean(xf * xf, axis=-1, keepdims=True) + EPS)
    return (y * g.astype(jnp.float32)).astype(x.dtype)


def layernorm(x, g, b):
    xf = x.astype(jnp.float32)
    mu = jnp.mean(xf, axis=-1, keepdims=True)
    var = jnp.mean(jnp.square(xf - mu), axis=-1, keepdims=True)
    y = (xf - mu) * lax.rsqrt(var + EPS)
    return (y * g.astype(jnp.float32) + b.astype(jnp.float32)).astype(x.dtype)


def depthwise_conv(x, w, b):
    pad = (CONV_WIDTH - 1) // 2
    y = lax.conv_general_dilated(
        x, w.astype(x.dtype), window_strides=(1,), padding=[(pad, pad)],
        dimension_numbers=("NWC", "WIO", "NWC"), feature_group_count=x.shape[-1])
    return y + b


def conformer_conv_branch(val, gate, conv_w, conv_b, ln_g, ln_b, w_out):
    c = val * jax.nn.sigmoid(gate)
    c = depthwise_conv(c, conv_w, conv_b)
    c = jax.nn.silu(layernorm(c, ln_g, ln_b))
    return c @ w_out


def spatial_gating_branch(u, v, ln_g, ln_b, w_s, b_s, w_out):
    bsz, seq, _ = v.shape
    n_chunks = seq // CHUNK
    vn = layernorm(v, ln_g, ln_b)
    vc = vn.reshape(bsz, n_chunks, CHUNK, SGU_GROUPS, SGU_GROUP_DIM)
    mixed = jnp.einsum("bcpgd,gqp->bcqgd", vc, w_s.astype(vc.dtype))
    mixed = mixed + jnp.transpose(b_s)[None, None, :, :, None]
    gated = u * mixed.reshape(bsz, seq, D_SGU)
    return gated @ w_out


def setup_inputs(seed: int = 0) -> dict:
    key = jax.random.key(seed)
    ks = jax.random.split(key, 24)
    f32 = jnp.float32

    def nrm(k, shape, scale):
        return jax.random.normal(k, shape, f32) * scale

    L = DEPTH
    return {
        "x": jax.random.normal(ks[0], (BATCH, SEQ, D_MODEL), f32),
        "norm_mix": 1.0 + nrm(ks[1], (L, D_MODEL), 0.02),
        "w_in": nrm(ks[2], (L, D_MODEL, D_IN), D_MODEL ** -0.5),
        "gate_bias": nrm(ks[3], (L, 2 * D_MODEL), 0.02),
        "conv_w": nrm(ks[4], (L, CONV_WIDTH, 1, D_CONV), CONV_WIDTH ** -0.5),
        "conv_b": nrm(ks[5], (L, D_CONV), 0.02),
        "conv_ln_g": 1.0 + nrm(ks[6], (L, D_CONV), 0.02),
        "conv_ln_b": nrm(ks[7], (L, D_CONV), 0.02),
        "w_conv_out": nrm(ks[8], (L, D_CONV, D_MODEL), D_CONV ** -0.5),
        "sgu_ln_g": 1.0 + nrm(ks[9], (L, D_SGU), 0.02),
        "sgu_ln_b": nrm(ks[10], (L, D_SGU), 0.02),
        "w_spatial": nrm(ks[11], (L, SGU_GROUPS, CHUNK, CHUNK), CHUNK ** -0.5),
        "b_spatial": 1.0 + nrm(ks[12], (L, SGU_GROUPS, CHUNK), 0.02),
        "w_sgu_out": nrm(ks[13], (L, D_SGU, D_MODEL), D_SGU ** -0.5),
        "w_o": nrm(ks[14], (L, D_MODEL, D_MODEL), D_MODEL ** -0.5),
        "norm_ffn": 1.0 + nrm(ks[15], (L, D_MODEL), 0.02),
        "w_ffn_gate": nrm(ks[16], (L, D_MODEL, D_FF), D_MODEL ** -0.5),
        "w_ffn_up": nrm(ks[17], (L, D_MODEL, D_FF), D_MODEL ** -0.5),
        "w_ffn_down": nrm(ks[18], (L, D_FF, D_MODEL), D_FF ** -0.5),
        "norm_final": 1.0 + nrm(ks[19], (D_MODEL,), 0.02),
    }


def reference(x, norm_mix, w_in, gate_bias, conv_w, conv_b, conv_ln_g, conv_ln_b,
              w_conv_out, sgu_ln_g, sgu_ln_b, w_spatial, b_spatial, w_sgu_out, w_o,
              norm_ffn, w_ffn_gate, w_ffn_up, w_ffn_down, norm_final):
    split_points = [D_CONV, 2 * D_CONV, 2 * D_CONV + D_SGU, 2 * D_CONV + 2 * D_SGU,
                    2 * D_CONV + 2 * D_SGU + D_MODEL]
    for l in range(DEPTH):
        h = rmsnorm(x, norm_mix[l])
        proj = h @ w_in[l]
        a_val, a_gate, u, v, g_a, g_b = jnp.split(proj, split_points, axis=-1)
        y_a = conformer_conv_branch(a_val, a_gate, conv_w[l], conv_b[l],
                                    conv_ln_g[l], conv_ln_b[l], w_conv_out[l])
        y_b = spatial_gating_branch(u, v, sgu_ln_g[l], sgu_ln_b[l],
                                    w_spatial[l], b_spatial[l], w_sgu_out[l])
        gb_a, gb_b = jnp.split(gate_bias[l], 2)
        merged = jax.nn.sigmoid(g_a + gb_a) * y_a + jax.nn.sigmoid(g_b + gb_b) * y_b
        x = x + merged @ w_o[l]
        h2 = rmsnorm(x, norm_ffn[l])
        x = x + (jax.nn.silu(h2 @ w_ffn_gate[l]) * (h2 @ w_ffn_up[l])) @ w_ffn_down[l]
    return rmsnorm(x, norm_final)
```

```python
import functools

import jax
import jax.numpy as jnp
from jax import lax
from jax.experimental import pallas as pl
from jax.experimental.pallas import tpu as pltpu

D_MODEL = 1024
CONV_WIDTH = 31
CONV_PAD = (CONV_WIDTH - 1) // 2
SGU_GROUPS = 8
SGU_GROUP_DIM = D_MODEL // SGU_GROUPS
CHUNK = 128
EPS = 1e-6

HALO = 16
MIX_TILE = 512
FFN_TILE = 512
GLU_COLS = 256
PROJ_COLS = 512
CONV_ROWS = 32
CONV_LANES = 256
FFN_COLS = 256
V7X_VMEM_LIMIT_BYTES = 60 * 1024 * 1024

F32 = jnp.float32
BF16 = jnp.bfloat16


def _rms(v, g):
    ms = jnp.mean(v * v, axis=-1, keepdims=True)
    return v * lax.rsqrt(ms + EPS) * g


def _layernorm(v, g, b):
    mu = jnp.mean(v, axis=-1, keepdims=True)
    d = v - mu
    var = jnp.mean(d * d, axis=-1, keepdims=True)
    return d * lax.rsqrt(var + EPS) * g + b


def _dot(a, b):
    return jnp.dot(a, b, preferred_element_type=F32)


def _mixer_kernel(x_ref, xp_ref, xn_ref, nrm_ref, w_in_ref, gbias_ref, cw_ref, cb_ref,
                  clg_ref, clb_ref, wco_ref, slg_ref, slb_ref, wsp_ref, bsp_ref,
                  wso_ref, wo_ref, o_ref,
                  h_ext, c_ext, proj, conv_o, act):
    t = pl.program_id(1)
    nt = pl.num_programs(1)
    tile = x_ref.shape[1]
    d = D_MODEL
    g = nrm_ref[...]

    h_ext[0:HALO, :] = _rms(xp_ref[0], g).astype(BF16)
    h_ext[HALO:HALO + tile, :] = _rms(x_ref[0], g).astype(BF16)
    h_ext[HALO + tile:, :] = _rms(xn_ref[0], g).astype(BF16)

    hv = h_ext[...]
    for j in range(d // GLU_COLS):
        cs = slice(j * GLU_COLS, (j + 1) * GLU_COLS)
        val = _dot(hv, w_in_ref[:, j * GLU_COLS:(j + 1) * GLU_COLS])
        gate = _dot(hv, w_in_ref[:, d + j * GLU_COLS:d + (j + 1) * GLU_COLS])
        c_ext[:, cs] = val * jax.nn.sigmoid(gate)
    c_ext[0:HALO, :] = jnp.where(t > 0, c_ext[0:HALO, :], 0.0)
    c_ext[HALO + tile:, :] = jnp.where(t < nt - 1, c_ext[HALO + tile:, :], 0.0)

    hm = h_ext[HALO:HALO + tile, :]
    for j in range(4 * d // PROJ_COLS):
        cs = slice(j * PROJ_COLS, (j + 1) * PROJ_COLS)
        proj[:, cs] = _dot(hm, w_in_ref[:, 2 * d + j * PROJ_COLS:2 * d + (j + 1) * PROJ_COLS])

    for rb in range(tile // CONV_ROWS):
        r0 = rb * CONV_ROWS
        for lc in range(d // CONV_LANES):
            ls = slice(lc * CONV_LANES, (lc + 1) * CONV_LANES)
            acc = jnp.broadcast_to(cb_ref[:, ls], (CONV_ROWS, CONV_LANES))
            for k in range(CONV_WIDTH):
                src = r0 + HALO - CONV_PAD + k
                acc = acc + cw_ref[k:k + 1, ls] * c_ext[src:src + CONV_ROWS, ls]
            conv_o[r0:r0 + CONV_ROWS, ls] = acc

    for rb in range(tile // CHUNK):
        rs = slice(rb * CHUNK, (rb + 1) * CHUNK)
        cn = _layernorm(conv_o[rs, :], clg_ref[...], clb_ref[...])
        act[rs, :] = (cn * jax.nn.sigmoid(cn)).astype(BF16)
    y_a = _dot(act[...], wco_ref[...])
    conv_o[...] = y_a

    n_chunks = tile // CHUNK
    for rb in range(n_chunks):
        rs = slice(rb * CHUNK, (rb + 1) * CHUNK)
        vn = _layernorm(proj[rs, d:2 * d], slg_ref[...], slb_ref[...])
        act[rs, :] = vn.astype(BF16)
    gated = []
    for gi in range(SGU_GROUPS):
        gs = slice(gi * SGU_GROUP_DIM, (gi + 1) * SGU_GROUP_DIM)
        rhs = jnp.concatenate(
            [act[c * CHUNK:(c + 1) * CHUNK, gs] for c in range(n_chunks)], axis=1)
        mixed = _dot(wsp_ref[gi], rhs)
        cols = []
        for c in range(n_chunks):
            m = mixed[:, c * SGU_GROUP_DIM:(c + 1) * SGU_GROUP_DIM] + bsp_ref[:, gs]
            u = proj[c * CHUNK:(c + 1) * CHUNK, gs]
            cols.append((u * m).astype(BF16))
        gated.append(cols)
    for gi in range(SGU_GROUPS):
        gs = slice(gi * SGU_GROUP_DIM, (gi + 1) * SGU_GROUP_DIM)
        for c in range(n_chunks):
            act[c * CHUNK:(c + 1) * CHUNK, gs] = gated[gi][c]
    y_b = _dot(act[...], wso_ref[...])

    for rb in range(n_chunks):
        rs = slice(rb * CHUNK, (rb + 1) * CHUNK)
        ga = jax.nn.sigmoid(proj[rs, 2 * d:3 * d] + gbias_ref[:, 0:d])
        gb = jax.nn.sigmoid(proj[rs, 3 * d:4 * d] + gbias_ref[:, d:2 * d])
        act[rs, :] = (ga * conv_o[rs, :] + gb * y_b[rs, :]).astype(BF16)
    o_ref[0] = x_ref[0] + _dot(act[...], wo_ref[...])


def _ffn_kernel(x_ref, nrm_ref, wg_ref, wu_ref, wd_ref, fin_ref, o_ref, hid, *, final_norm):
    x = x_ref[...]
    h = _rms(x, nrm_ref[...]).astype(BF16)
    d_ff = wg_ref.shape[1]
    for j in range(d_ff // FFN_COLS):
        cs = slice(j * FFN_COLS, (j + 1) * FFN_COLS)
        gt = _dot(h, wg_ref[:, cs])
        up = _dot(h, wu_ref[:, cs])
        hid[:, cs] = (gt * jax.nn.sigmoid(gt) * up).astype(BF16)
    y = x + _dot(hid[...], wd_ref[...])
    if final_norm:
        y = _rms(y, fin_ref[...])
    o_ref[...] = y


def _resident(shape):
    nd = len(shape)
    return pl.BlockSpec(shape, lambda *_: (0,) * nd, pipeline_mode=pl.Buffered(1))


def _mixer(x, nrm, w_in, gbias, cw, cb, clg, clb, wco, slg, slb, wsp, bsp, wso, wo):
    bsz, seq, d = x.shape
    tile = MIX_TILE
    assert seq % tile == 0 and tile % CHUNK == 0 and tile % HALO == 0
    nt = seq // tile
    hb = tile // HALO
    last_hb = seq // HALO - 1
    params = (nrm, w_in, gbias, cw, cb, clg, clb, wco, slg, slb, wsp, bsp, wso, wo)
    in_specs = [
        pl.BlockSpec((1, tile, d), lambda b, t: (b, t, 0)),
        pl.BlockSpec((1, HALO, d), lambda b, t: (b, jnp.maximum(t * hb - 1, 0), 0)),
        pl.BlockSpec((1, HALO, d), lambda b, t: (b, jnp.minimum((t + 1) * hb, last_hb), 0)),
    ] + [_resident(p.shape) for p in params]
    ext = tile + 2 * HALO
    return pl.pallas_call(
        _mixer_kernel,
        grid=(bsz, nt),
        in_specs=in_specs,
        out_specs=pl.BlockSpec((1, tile, d), lambda b, t: (b, t, 0)),
        out_shape=jax.ShapeDtypeStruct(x.shape, x.dtype),
        scratch_shapes=[
            pltpu.VMEM((ext, d), BF16),
            pltpu.VMEM((ext, d), F32),
            pltpu.VMEM((tile, 4 * d), F32),
            pltpu.VMEM((tile, d), F32),
            pltpu.VMEM((tile, d), BF16),
        ],
        compiler_params=pltpu.CompilerParams(
            dimension_semantics=("arbitrary", "arbitrary"),
            vmem_limit_bytes=V7X_VMEM_LIMIT_BYTES),
        name="mixer",
    )(x, x, x, *params)


def _ffn(x2, nrm, wg, wu, wd, fin, final_norm):
    n, d = x2.shape
    tile = FFN_TILE
    assert n % tile == 0 and wg.shape[1] % FFN_COLS == 0
    params = (nrm, wg, wu, wd, fin)
    return pl.pallas_call(
        functools.partial(_ffn_kernel, final_norm=final_norm),
        grid=(n // tile,),
        in_specs=[pl.BlockSpec((tile, d), lambda i: (i, 0))] + [_resident(p.shape) for p in params],
        out_specs=pl.BlockSpec((tile, d), lambda i: (i, 0)),
        out_shape=jax.ShapeDtypeStruct(x2.shape, x2.dtype),
        scratch_shapes=[pltpu.VMEM((tile, wg.shape[1]), BF16)],
        compiler_params=pltpu.CompilerParams(
            dimension_semantics=("arbitrary",),
            vmem_limit_bytes=V7X_VMEM_LIMIT_BYTES),
        name="ffn",
    )(x2, *params)


def kernel(x, norm_mix, w_in, gate_bias, conv_w, conv_b, conv_ln_g, conv_ln_b, w_conv_out, sgu_ln_g, sgu_ln_b, w_spatial, b_spatial, w_sgu_out, w_o, norm_ffn, w_ffn_gate, w_ffn_up, w_ffn_down, norm_final):
    bsz, seq, d = x.shape
    depth = w_in.shape[0]
    row = lambda v: v.reshape(1, -1).astype(F32)
    fin = row(norm_final)
    for l in range(depth):
        bsp = jnp.repeat(jnp.transpose(b_spatial[l]).astype(F32), SGU_GROUP_DIM, axis=1)
        x = _mixer(
            x, row(norm_mix[l]), w_in[l].astype(BF16), row(gate_bias[l]),
            conv_w[l].reshape(CONV_WIDTH, d).astype(F32), row(conv_b[l]),
            row(conv_ln_g[l]), row(conv_ln_b[l]), w_conv_out[l].astype(BF16),
            row(sgu_ln_g[l]), row(sgu_ln_b[l]), w_spatial[l].astype(BF16), bsp,
            w_sgu_out[l].astype(BF16), w_o[l].astype(BF16))
        x = _ffn(
            x.reshape(bsz * seq, d), row(norm_ffn[l]), w_ffn_gate[l].astype(BF16),
            w_ffn_up[l].astype(BF16), w_ffn_down[l].astype(BF16), fin,
            final_norm=(l == depth - 1)).reshape(bsz, seq, d)
    return x
```

```python
import functools

import jax
import jax.numpy as jnp
from jax import lax
from jax.experimental import pallas as pl
from jax.experimental.pallas import tpu as pltpu

D_MODEL = 1024
CONV_WIDTH = 31
CONV_PAD = (CONV_WIDTH - 1) // 2
SGU_GROUPS = 8
SGU_GROUP_DIM = D_MODEL // SGU_GROUPS
CHUNK = 128
EPS = 1e-6

LANES = 128
SUBLANES = 8
HALO = 16
C_OFF = 16
MIX_TILE = 512
FFN_TILE = 512
GLU_COLS = 256
PROJ_COLS = 512
FFN_COLS = 256
V7X_VMEM_LIMIT_BYTES = 60 * 1024 * 1024

F32 = jnp.float32
BF16 = jnp.bfloat16


def _rms(v, g):
    ms = jnp.mean(v * v, axis=-1, keepdims=True)
    return v * lax.rsqrt(ms + EPS) * g


def _layernorm(v, g, b):
    mu = jnp.mean(v, axis=-1, keepdims=True)
    d = v - mu
    var = jnp.mean(d * d, axis=-1, keepdims=True)
    return d * lax.rsqrt(var + EPS) * g + b


def _dot(a, b):
    return jnp.dot(a, b, preferred_element_type=F32)


def _mixer_kernel(x_ref, xp_ref, xn_ref, nrm_ref, w_in_ref, gbias_ref, cw_ref, cb_ref,
                  clg_ref, clb_ref, wco_ref, slg_ref, slb_ref, wsp_ref, bsp_ref,
                  wso_ref, wo_ref, o_ref,
                  h_ext, c_pad, proj, conv_o, y_a, vn_s, act):
    t = pl.program_id(1)
    nt = pl.num_programs(1)
    tile = x_ref.shape[1]
    ext = tile + 2 * HALO
    pitch = ext // SUBLANES
    n_slabs = D_MODEL // LANES
    d = D_MODEL
    g = nrm_ref[...]

    h_ext[0:HALO, :] = _rms(xp_ref[0], g).astype(BF16)
    h_ext[HALO:HALO + tile, :] = _rms(x_ref[0], g).astype(BF16)
    h_ext[HALO + tile:, :] = _rms(xn_ref[0], g).astype(BF16)

    hv = h_ext[...]
    for j in range(d // GLU_COLS):
        val = _dot(hv, w_in_ref[:, j * GLU_COLS:(j + 1) * GLU_COLS])
        gate = _dot(hv, w_in_ref[:, d + j * GLU_COLS:d + (j + 1) * GLU_COLS])
        c = val * jax.nn.sigmoid(gate)
        for i in range(GLU_COLS // LANES):
            c_pad[j * (GLU_COLS // LANES) + i, C_OFF:C_OFF + ext, :] = c[:, i * LANES:(i + 1) * LANES]
    lo = slice(C_OFF, C_OFF + HALO)
    hi = slice(C_OFF + HALO + tile, C_OFF + ext)
    c_pad[:, lo, :] = jnp.where(t > 0, c_pad[:, lo, :], 0.0)
    c_pad[:, hi, :] = jnp.where(t < nt - 1, c_pad[:, hi, :], 0.0)
    c_pad[:, 0:C_OFF, :] = jnp.zeros((n_slabs, C_OFF, LANES), F32)
    c_pad[:, C_OFF + ext:, :] = jnp.zeros((n_slabs, C_OFF, LANES), F32)

    hm = h_ext[HALO:HALO + tile, :]
    for j in range(4 * d // PROJ_COLS):
        cs = slice(j * PROJ_COLS, (j + 1) * PROJ_COLS)
        proj[:, cs] = _dot(hm, w_in_ref[:, 2 * d + j * PROJ_COLS:2 * d + (j + 1) * PROJ_COLS])

    for s in range(n_slabs):
        ls = slice(s * LANES, (s + 1) * LANES)
        wk = [jnp.broadcast_to(cw_ref[k:k + 1, ls], (SUBLANES, LANES)) for k in range(CONV_WIDTH)]
        bias = jnp.broadcast_to(cb_ref[:, ls], (SUBLANES, LANES))
        taps = {}
        for m in range(pitch):
            acc = bias
            for k in range(CONV_WIDTH):
                v = m + k - CONV_PAD
                if v not in taps:
                    taps[v] = c_pad[s, pl.ds(C_OFF + v, SUBLANES, stride=pitch), :]
                acc = acc + wk[k] * taps[v]
            conv_o[s, pl.ds(m, SUBLANES, stride=pitch), :] = acc

    n_chunks = tile // CHUNK
    for rb in range(n_chunks):
        rs = slice(rb * CHUNK, (rb + 1) * CHUNK)
        es = slice(HALO + rb * CHUNK, HALO + (rb + 1) * CHUNK)
        cv = jnp.concatenate([conv_o[s, es, :] for s in range(n_slabs)], axis=1)
        cn = _layernorm(cv, clg_ref[...], clb_ref[...])
        act[rs, :] = (cn * jax.nn.sigmoid(cn)).astype(BF16)
    y_a[...] = _dot(act[...], wco_ref[...])

    for rb in range(n_chunks):
        rs = slice(rb * CHUNK, (rb + 1) * CHUNK)
        vn = _layernorm(proj[rs, d:2 * d], slg_ref[...], slb_ref[...])
        vn_s[rs, :] = vn.astype(BF16)
    for gi in range(SGU_GROUPS):
        gs = slice(gi * SGU_GROUP_DIM, (gi + 1) * SGU_GROUP_DIM)
        rhs = jnp.concatenate(
            [vn_s[c * CHUNK:(c + 1) * CHUNK, gs] for c in range(n_chunks)], axis=1)
        mixed = _dot(wsp_ref[gi], rhs)
        for c in range(n_chunks):
            m = mixed[:, c * SGU_GROUP_DIM:(c + 1) * SGU_GROUP_DIM] + bsp_ref[:, gs]
            u = proj[c * CHUNK:(c + 1) * CHUNK, gs]
            act[c * CHUNK:(c + 1) * CHUNK, gs] = (u * m).astype(BF16)
    y_b = _dot(act[...], wso_ref[...])

    for rb in range(n_chunks):
        rs = slice(rb * CHUNK, (rb + 1) * CHUNK)
        ga = jax.nn.sigmoid(proj[rs, 2 * d:3 * d] + gbias_ref[:, 0:d])
        gb = jax.nn.sigmoid(proj[rs, 3 * d:4 * d] + gbias_ref[:, d:2 * d])
        act[rs, :] = (ga * y_a[rs, :] + gb * y_b[rs, :]).astype(BF16)
    o_ref[0] = x_ref[0] + _dot(act[...], wo_ref[...])


def _ffn_kernel(x_ref, nrm_ref, wg_ref, wu_ref, wd_ref, fin_ref, o_ref, hid, *, final_norm):
    x = x_ref[...]
    h = _rms(x, nrm_ref[...]).astype(BF16)
    d_ff = wg_ref.shape[1]
    for j in range(d_ff // FFN_COLS):
        cs = slice(j * FFN_COLS, (j + 1) * FFN_COLS)
        gt = _dot(h, wg_ref[:, cs])
        up = _dot(h, wu_ref[:, cs])
        hid[:, cs] = (gt * jax.nn.sigmoid(gt) * up).astype(BF16)
    y = x + _dot(hid[...], wd_ref[...])
    if final_norm:
        y = _rms(y, fin_ref[...])
    o_ref[...] = y


def _resident(shape):
    nd = len(shape)
    return pl.BlockSpec(shape, lambda *_: (0,) * nd, pipeline_mode=pl.Buffered(1))


def _mixer(x, nrm, w_in, gbias, cw, cb, clg, clb, wco, slg, slb, wsp, bsp, wso, wo):
    bsz, seq, d = x.shape
    tile = MIX_TILE
    ext = tile + 2 * HALO
    assert seq % tile == 0 and tile % CHUNK == 0 and tile % HALO == 0
    assert ext % SUBLANES == 0 and (ext // SUBLANES) % 8 != 0
    assert CONV_PAD <= C_OFF and CONV_PAD <= HALO
    nt = seq // tile
    hb = tile // HALO
    last_hb = seq // HALO - 1
    n_slabs = d // LANES
    params = (nrm, w_in, gbias, cw, cb, clg, clb, wco, slg, slb, wsp, bsp, wso, wo)
    in_specs = [
        pl.BlockSpec((1, tile, d), lambda b, t: (b, t, 0)),
        pl.BlockSpec((1, HALO, d), lambda b, t: (b, jnp.maximum(t * hb - 1, 0), 0)),
        pl.BlockSpec((1, HALO, d), lambda b, t: (b, jnp.minimum((t + 1) * hb, last_hb), 0)),
    ] + [_resident(p.shape) for p in params]
    return pl.pallas_call(
        _mixer_kernel,
        grid=(bsz, nt),
        in_specs=in_specs,
        out_specs=pl.BlockSpec((1, tile, d), lambda b, t: (b, t, 0)),
        out_shape=jax.ShapeDtypeStruct(x.shape, x.dtype),
        scratch_shapes=[
            pltpu.VMEM((ext, d), BF16),
            pltpu.VMEM((n_slabs, ext + 2 * C_OFF, LANES), F32),
            pltpu.VMEM((tile, 4 * d), F32),
            pltpu.VMEM((n_slabs, ext, LANES), F32),
            pltpu.VMEM((tile, d), F32),
            pltpu.VMEM((tile, d), BF16),
            pltpu.VMEM((tile, d), BF16),
        ],
        compiler_params=pltpu.CompilerParams(
            dimension_semantics=("arbitrary", "arbitrary"),
            vmem_limit_bytes=V7X_VMEM_LIMIT_BYTES),
        name="mixer",
    )(x, x, x, *params)


def _ffn(x2, nrm, wg, wu, wd, fin, final_norm):
    n, d = x2.shape
    tile = FFN_TILE
    assert n % tile == 0 and wg.shape[1] % FFN_COLS == 0
    params = (nrm, wg, wu, wd, fin)
    return pl.pallas_call(
        functools.partial(_ffn_kernel, final_norm=final_norm),
        grid=(n // tile,),
        in_specs=[pl.BlockSpec((tile, d), lambda i: (i, 0))] + [_resident(p.shape) for p in params],
        out_specs=pl.BlockSpec((tile, d), lambda i: (i, 0)),
        out_shape=jax.ShapeDtypeStruct(x2.shape, x2.dtype),
        scratch_shapes=[pltpu.VMEM((tile, wg.shape[1]), BF16)],
        compiler_params=pltpu.CompilerParams(
            dimension_semantics=("arbitrary",),
            vmem_limit_bytes=V7X_VMEM_LIMIT_BYTES),
        name="ffn",
    )(x2, *params)


def kernel(x, norm_mix, w_in, gate_bias, conv_w, conv_b, conv_ln_g, conv_ln_b, w_conv_out, sgu_ln_g, sgu_ln_b, w_spatial, b_spatial, w_sgu_out, w_o, norm_ffn, w_ffn_gate, w_ffn_up, w_ffn_down, norm_final):
    bsz, seq, d = x.shape
    depth = w_in.shape[0]
    row = lambda v: v.reshape(1, -1).astype(F32)
    fin = row(norm_final)
    for l in range(depth):
        bsp = jnp.repeat(jnp.transpose(b_spatial[l]).astype(F32), SGU_GROUP_DIM, axis=1)
        x = _mixer(
            x, row(norm_mix[l]), w_in[l].astype(BF16), row(gate_bias[l]),
            conv_w[l].reshape(CONV_WIDTH, d).astype(F32), row(conv_b[l]),
            row(conv_ln_g[l]), row(conv_ln_b[l]), w_conv_out[l].astype(BF16),
            row(sgu_ln_g[l]), row(sgu_ln_b[l]), w_spatial[l].astype(BF16), bsp,
            w_sgu_out[l].astype(BF16), w_o[l].astype(BF16))
        x = _ffn(
            x.reshape(bsz * seq, d), row(norm_ffn[l]), w_ffn_gate[l].astype(BF16),
            w_ffn_up[l].astype(BF16), w_ffn_down[l].astype(BF16), fin,
            final_norm=(l == depth - 1)).reshape(bsz, seq, d)
    return x
```

```python
import functools

import jax
import jax.numpy as jnp
from jax import lax
from jax.experimental import pallas as pl
from jax.experimental.pallas import tpu as pltpu

D_MODEL = 1024
CONV_WIDTH = 31
CONV_PAD = (CONV_WIDTH - 1) // 2
SGU_GROUPS = 8
SGU_GROUP_DIM = D_MODEL // SGU_GROUPS
CHUNK = 128
EPS = 1e-6

LANES = 128
SUBLANES = 8
HALO = 16
C_OFF = 16
MIX_TILE = 512
FFN_TILE = 1024
GLU_COLS = 256
PROJ_COLS = 512
GROUPS_PER_PROJ = PROJ_COLS // SGU_GROUP_DIM
SLABS_PER_ITER = 4
FFN_COLS = 256
V7X_VMEM_LIMIT_BYTES = 60 * 1024 * 1024

F32 = jnp.float32
BF16 = jnp.bfloat16


def _rms(v, g):
    ms = jnp.mean(v * v, axis=-1, keepdims=True)
    return v * lax.rsqrt(ms + EPS) * g


def _layernorm(v, g, b):
    mu = jnp.mean(v, axis=-1, keepdims=True)
    d = v - mu
    var = jnp.mean(d * d, axis=-1, keepdims=True)
    return d * lax.rsqrt(var + EPS) * g + b


def _dot(a, b):
    return jnp.dot(a, b, preferred_element_type=F32)


def _proj_cols(proj, part, rs):
    per = D_MODEL // PROJ_COLS
    return jnp.concatenate([proj[part * per + i, rs, :] for i in range(per)], axis=1)


def _mixer_kernel(x_ref, xp_ref, xn_ref, nrm_ref, w_glu_ref, w_proj_ref, gbias_ref, cw_ref, cb_ref,
                  clg_ref, clb_ref, wco_ref, slg_ref, slb_ref, wsp_ref, bsp_ref,
                  wso_ref, wo_ref, o_ref,
                  h_ext, c_pad, proj, conv_o, y_a, vn_s, act):
    t = pl.program_id(1)
    nt = pl.num_programs(1)
    tile = x_ref.shape[1]
    ext = tile + 2 * HALO
    pitch = ext // SUBLANES
    n_slabs = D_MODEL // LANES
    d = D_MODEL
    g = nrm_ref[...]

    h_ext[0:HALO, :] = _rms(xp_ref[0], g).astype(BF16)
    h_ext[HALO:HALO + tile, :] = _rms(x_ref[0], g).astype(BF16)
    h_ext[HALO + tile:, :] = _rms(xn_ref[0], g).astype(BF16)

    hv = h_ext[...]
    for j in range(d // GLU_COLS):
        val = _dot(hv, w_glu_ref[:, j * GLU_COLS:(j + 1) * GLU_COLS])
        gate = _dot(hv, w_glu_ref[:, d + j * GLU_COLS:d + (j + 1) * GLU_COLS])
        c = val * jax.nn.sigmoid(gate)
        for i in range(GLU_COLS // LANES):
            c_pad[j * (GLU_COLS // LANES) + i, C_OFF:C_OFF + ext, :] = c[:, i * LANES:(i + 1) * LANES]
    lo = slice(C_OFF, C_OFF + HALO)
    hi = slice(C_OFF + HALO + tile, C_OFF + ext)
    c_pad[:, lo, :] = jnp.where(t > 0, c_pad[:, lo, :], 0.0)
    c_pad[:, hi, :] = jnp.where(t < nt - 1, c_pad[:, hi, :], 0.0)
    c_pad[:, 0:C_OFF, :] = jnp.zeros((n_slabs, C_OFF, LANES), F32)
    c_pad[:, C_OFF + ext:, :] = jnp.zeros((n_slabs, C_OFF, LANES), F32)

    def conv_proj_body(it, carry):
        for sub in range(SLABS_PER_ITER):
            s = it * SLABS_PER_ITER + sub
            proj[s] = _dot(h_ext[HALO:HALO + tile, :], w_proj_ref[s])
            wk = [jnp.broadcast_to(cw_ref[s, k:k + 1, :], (SUBLANES, LANES)) for k in range(CONV_WIDTH)]
            bias = jnp.broadcast_to(cb_ref[s], (SUBLANES, LANES))
            taps = {}
            for m in range(pitch):
                acc = bias
                for k in range(CONV_WIDTH):
                    v = m + k - CONV_PAD
                    if v not in taps:
                        taps[v] = c_pad[s, pl.ds(C_OFF + v, SUBLANES, stride=pitch), :]
                    acc = acc + wk[k] * taps[v]
                conv_o[s, pl.ds(m, SUBLANES, stride=pitch), :] = acc
        return carry
    lax.fori_loop(0, n_slabs // SLABS_PER_ITER, conv_proj_body, 0)

    n_chunks = tile // CHUNK
    for rb in range(n_chunks):
        rs = slice(rb * CHUNK, (rb + 1) * CHUNK)
        es = slice(HALO + rb * CHUNK, HALO + (rb + 1) * CHUNK)
        cv = jnp.concatenate([conv_o[s, es, :] for s in range(n_slabs)], axis=1)
        cn = _layernorm(cv, clg_ref[...], clb_ref[...])
        act[rs, :] = (cn * jax.nn.sigmoid(cn)).astype(BF16)
    y_a[...] = _dot(act[...], wco_ref[...])

    for rb in range(n_chunks):
        rs = slice(rb * CHUNK, (rb + 1) * CHUNK)
        vn = _layernorm(_proj_cols(proj, 1, rs), slg_ref[...], slb_ref[...])
        vn_s[rs, :] = vn.astype(BF16)
    for gi in range(SGU_GROUPS):
        gs = slice(gi * SGU_GROUP_DIM, (gi + 1) * SGU_GROUP_DIM)
        rhs = jnp.concatenate(
            [vn_s[c * CHUNK:(c + 1) * CHUNK, gs] for c in range(n_chunks)], axis=1)
        mixed = _dot(wsp_ref[gi], rhs)
        for c in range(n_chunks):
            m = mixed[:, c * SGU_GROUP_DIM:(c + 1) * SGU_GROUP_DIM] + bsp_ref[:, gs]
            u = proj[gi // GROUPS_PER_PROJ, c * CHUNK:(c + 1) * CHUNK,
                     (gi % GROUPS_PER_PROJ) * SGU_GROUP_DIM:(gi % GROUPS_PER_PROJ + 1) * SGU_GROUP_DIM]
            act[c * CHUNK:(c + 1) * CHUNK, gs] = (u * m).astype(BF16)
    y_b = _dot(act[...], wso_ref[...])

    for rb in range(n_chunks):
        rs = slice(rb * CHUNK, (rb + 1) * CHUNK)
        ga = jax.nn.sigmoid(_proj_cols(proj, 2, rs) + gbias_ref[:, 0:d])
        gb = jax.nn.sigmoid(_proj_cols(proj, 3, rs) + gbias_ref[:, d:2 * d])
        act[rs, :] = (ga * y_a[rs, :] + gb * y_b[rs, :]).astype(BF16)
    o_ref[0] = x_ref[0] + _dot(act[...], wo_ref[...])


def _ffn_kernel(x_ref, nrm_ref, wg_ref, wu_ref, wd_ref, fin_ref, o_ref, hid, *, final_norm):
    x = x_ref[...]
    h = _rms(x, nrm_ref[...]).astype(BF16)
    d_ff = wg_ref.shape[1]
    for j in range(d_ff // FFN_COLS):
        cs = slice(j * FFN_COLS, (j + 1) * FFN_COLS)
        gt = _dot(h, wg_ref[:, cs])
        up = _dot(h, wu_ref[:, cs])
        hid[:, cs] = (gt * jax.nn.sigmoid(gt) * up).astype(BF16)
    y = x + _dot(hid[...], wd_ref[...])
    if final_norm:
        y = _rms(y, fin_ref[...])
    o_ref[...] = y


def _resident(shape):
    nd = len(shape)
    return pl.BlockSpec(shape, lambda *_: (0,) * nd, pipeline_mode=pl.Buffered(1))


def _mixer(x, nrm, w_glu, w_proj, gbias, cw, cb, clg, clb, wco, slg, slb, wsp, bsp, wso, wo):
    bsz, seq, d = x.shape
    tile = MIX_TILE
    ext = tile + 2 * HALO
    assert seq % tile == 0 and tile % CHUNK == 0 and tile % HALO == 0
    assert ext % SUBLANES == 0 and (ext // SUBLANES) % 8 != 0
    assert CONV_PAD <= C_OFF and CONV_PAD <= HALO
    nt = seq // tile
    hb = tile // HALO
    last_hb = seq // HALO - 1
    n_slabs = d // LANES
    assert 4 * d // PROJ_COLS == n_slabs and n_slabs % SLABS_PER_ITER == 0
    params = (nrm, w_glu, w_proj, gbias, cw, cb, clg, clb, wco, slg, slb, wsp, bsp, wso, wo)
    in_specs = [
        pl.BlockSpec((1, tile, d), lambda b, t: (b, t, 0)),
        pl.BlockSpec((1, HALO, d), lambda b, t: (b, jnp.maximum(t * hb - 1, 0), 0)),
        pl.BlockSpec((1, HALO, d), lambda b, t: (b, jnp.minimum((t + 1) * hb, last_hb), 0)),
    ] + [_resident(p.shape) for p in params]
    return pl.pallas_call(
        _mixer_kernel,
        grid=(bsz, nt),
        in_specs=in_specs,
        out_specs=pl.BlockSpec((1, tile, d), lambda b, t: (b, t, 0)),
        out_shape=jax.ShapeDtypeStruct(x.shape, x.dtype),
        scratch_shapes=[
            pltpu.VMEM((ext, d), BF16),
            pltpu.VMEM((n_slabs, ext + 2 * C_OFF, LANES), F32),
            pltpu.VMEM((4 * d // PROJ_COLS, tile, PROJ_COLS), F32),
            pltpu.VMEM((n_slabs, ext, LANES), F32),
            pltpu.VMEM((tile, d), F32),
            pltpu.VMEM((tile, d), BF16),
            pltpu.VMEM((tile, d), BF16),
        ],
        compiler_params=pltpu.CompilerParams(
            dimension_semantics=("arbitrary", "arbitrary"),
            vmem_limit_bytes=V7X_VMEM_LIMIT_BYTES),
        name="mixer",
    )(x, x, x, *params)


def _ffn(x2, nrm, wg, wu, wd, fin, final_norm):
    n, d = x2.shape
    tile = FFN_TILE
    assert n % tile == 0 and wg.shape[1] % FFN_COLS == 0
    params = (nrm, wg, wu, wd, fin)
    return pl.pallas_call(
        functools.partial(_ffn_kernel, final_norm=final_norm),
        grid=(n // tile,),
        in_specs=[pl.BlockSpec((tile, d), lambda i: (i, 0))] + [_resident(p.shape) for p in params],
        out_specs=pl.BlockSpec((tile, d), lambda i: (i, 0)),
        out_shape=jax.ShapeDtypeStruct(x2.shape, x2.dtype),
        scratch_shapes=[pltpu.VMEM((tile, wg.shape[1]), BF16)],
        compiler_params=pltpu.CompilerParams(
            dimension_semantics=("arbitrary",),
            vmem_limit_bytes=V7X_VMEM_LIMIT_BYTES),
        name="ffn",
    )(x2, *params)


def _mixer_layer(x, l, norm_mix, w_in, gate_bias, conv_w, conv_b, conv_ln_g, conv_ln_b, w_conv_out,
                 sgu_ln_g, sgu_ln_b, w_spatial, b_spatial, w_sgu_out, w_o):
    d = x.shape[-1]
    n_slabs = d // LANES
    row = lambda v: v.reshape(1, -1).astype(F32)
    w = w_in[l].astype(BF16)
    w_glu = w[:, :2 * d]
    w_proj = w[:, 2 * d:].reshape(d, 4 * d // PROJ_COLS, PROJ_COLS).transpose(1, 0, 2)
    cw = conv_w[l].reshape(CONV_WIDTH, n_slabs, LANES).transpose(1, 0, 2).astype(F32)
    cb = conv_b[l].reshape(n_slabs, 1, LANES).astype(F32)
    bsp = jnp.repeat(jnp.transpose(b_spatial[l]).astype(F32), SGU_GROUP_DIM, axis=1)
    return _mixer(
        x, row(norm_mix[l]), w_glu, w_proj, row(gate_bias[l]), cw, cb,
        row(conv_ln_g[l]), row(conv_ln_b[l]), w_conv_out[l].astype(BF16),
        row(sgu_ln_g[l]), row(sgu_ln_b[l]), w_spatial[l].astype(BF16), bsp,
        w_sgu_out[l].astype(BF16), w_o[l].astype(BF16))


def kernel(x, norm_mix, w_in, gate_bias, conv_w, conv_b, conv_ln_g, conv_ln_b, w_conv_out, sgu_ln_g, sgu_ln_b, w_spatial, b_spatial, w_sgu_out, w_o, norm_ffn, w_ffn_gate, w_ffn_up, w_ffn_down, norm_final):
    bsz, seq, d = x.shape
    depth = w_in.shape[0]
    row = lambda v: v.reshape(1, -1).astype(F32)
    fin = row(norm_final)
    for l in range(depth):
        x = _mixer_layer(x, l, norm_mix, w_in, gate_bias, conv_w, conv_b, conv_ln_g, conv_ln_b,
                         w_conv_out, sgu_ln_g, sgu_ln_b, w_spatial, b_spatial, w_sgu_out, w_o)
        x = _ffn(
            x.reshape(bsz * seq, d), row(norm_ffn[l]), w_ffn_gate[l].astype(BF16),
            w_ffn_up[l].astype(BF16), w_ffn_down[l].astype(BF16), fin,
            final_norm=(l == depth - 1)).reshape(bsz, seq, d)
    return x
```

```python
import functools

import jax
import jax.numpy as jnp
from jax import lax
from jax.experimental import pallas as pl
from jax.experimental.pallas import tpu as pltpu

D_MODEL = 1024
CONV_WIDTH = 31
CONV_PAD = (CONV_WIDTH - 1) // 2
SGU_GROUPS = 8
SGU_GROUP_DIM = D_MODEL // SGU_GROUPS
CHUNK = 128
EPS = 1e-6

LANES = 128
SUBLANES = 8
PACKED = 16
HALO = 16
C_OFF = 16
MIX_TILE = 512
FFN_TILE = 1024
GLU_COLS = 256
PROJ_COLS = 512
FFN_COLS = 256
V7X_VMEM_LIMIT_BYTES = 60 * 1024 * 1024

F32 = jnp.float32
BF16 = jnp.bfloat16


def _rms(v, g):
    ms = jnp.mean(v * v, axis=-1, keepdims=True)
    return v * lax.rsqrt(ms + EPS) * g


def _layernorm(v, g, b):
    mu = jnp.mean(v, axis=-1, keepdims=True)
    d = v - mu
    var = jnp.mean(d * d, axis=-1, keepdims=True)
    return d * lax.rsqrt(var + EPS) * g + b


def _dot(a, b):
    return jnp.dot(a, b, preferred_element_type=F32)


def _mixer_kernel(x_ref, xp_ref, xn_ref, nrm_ref, w_in_ref, gbias_ref, cw_ref, cb_ref,
                  clg_ref, clb_ref, wco_ref, slg_ref, slb_ref, wsp_ref, bsp_ref,
                  wso_ref, wo_ref, o_ref,
                  h_ext, c_pad, c_rows, proj, conv_o, y_a, vn_s, act):
    t = pl.program_id(1)
    nt = pl.num_programs(1)
    tile = x_ref.shape[1]
    ext = tile + 2 * HALO
    pitch = ext // PACKED
    n_slabs = D_MODEL // LANES
    d = D_MODEL
    g = nrm_ref[...]

    h_ext[0:HALO, :] = _rms(xp_ref[0], g).astype(BF16)
    h_ext[HALO:HALO + tile, :] = _rms(x_ref[0], g).astype(BF16)
    h_ext[HALO + tile:, :] = _rms(xn_ref[0], g).astype(BF16)

    hv = h_ext[...]
    for j in range(d // GLU_COLS):
        val = _dot(hv, w_in_ref[:, j * GLU_COLS:(j + 1) * GLU_COLS])
        gate = _dot(hv, w_in_ref[:, d + j * GLU_COLS:d + (j + 1) * GLU_COLS])
        c = val * jax.nn.sigmoid(gate)
        for i in range(GLU_COLS // LANES):
            c_pad[j * (GLU_COLS // LANES) + i, C_OFF:C_OFF + ext, :] = c[:, i * LANES:(i + 1) * LANES]
    lo = slice(C_OFF, C_OFF + HALO)
    hi = slice(C_OFF + HALO + tile, C_OFF + ext)
    c_pad[:, lo, :] = jnp.where(t > 0, c_pad[:, lo, :], 0.0)
    c_pad[:, hi, :] = jnp.where(t < nt - 1, c_pad[:, hi, :], 0.0)
    c_pad[:, 0:C_OFF, :] = jnp.zeros((n_slabs, C_OFF, LANES), F32)
    c_pad[:, C_OFF + ext:, :] = jnp.zeros((n_slabs, C_OFF, LANES), F32)

    for s in range(n_slabs):
        for v in range(-CONV_PAD, pitch + CONV_PAD):
            top = c_pad[s, pl.ds(C_OFF + v, SUBLANES, stride=pitch), :]
            bot = c_pad[s, pl.ds(C_OFF + v + SUBLANES * pitch, SUBLANES, stride=pitch), :]
            c_rows[s, (v + CONV_PAD) * PACKED:(v + CONV_PAD + 1) * PACKED, :] = (
                jnp.concatenate([top, bot], axis=0).astype(BF16))

    def conv_region(_, carry):
        for s in range(n_slabs):
            bias = jnp.broadcast_to(cb_ref[s], (PACKED, LANES))
            wk = [cw_ref[s, k * PACKED:(k + 1) * PACKED, :] for k in range(CONV_WIDTH)]
            for m in range(pitch):
                acc = jnp.zeros((PACKED, LANES), F32)
                for k in range(CONV_WIDTH):
                    acc = acc + (wk[k].astype(F32)
                                 * c_rows[s, (m + k) * PACKED:(m + k + 1) * PACKED, :].astype(F32))
                acc = acc + bias
                conv_o[s, pl.ds(m, SUBLANES, stride=pitch), :] = acc[0:SUBLANES]
                conv_o[s, pl.ds(m + SUBLANES * pitch, SUBLANES, stride=pitch), :] = acc[SUBLANES:]
        return carry
    lax.fori_loop(0, jnp.minimum(t + 1, 1), conv_region, 0)

    hm = h_ext[HALO:HALO + tile, :]
    for j in range(4 * d // PROJ_COLS):
        cs = slice(j * PROJ_COLS, (j + 1) * PROJ_COLS)
        proj[:, cs] = _dot(hm, w_in_ref[:, 2 * d + j * PROJ_COLS:2 * d + (j + 1) * PROJ_COLS])

    n_chunks = tile // CHUNK
    for rb in range(n_chunks):
        rs = slice(rb * CHUNK, (rb + 1) * CHUNK)
        es = slice(HALO + rb * CHUNK, HALO + (rb + 1) * CHUNK)
        cv = jnp.concatenate([conv_o[s, es, :] for s in range(n_slabs)], axis=1)
        cn = _layernorm(cv, clg_ref[...], clb_ref[...])
        act[rs, :] = (cn * jax.nn.sigmoid(cn)).astype(BF16)
    y_a[...] = _dot(act[...], wco_ref[...])

    for rb in range(n_chunks):
        rs = slice(rb * CHUNK, (rb + 1) * CHUNK)
        vn = _layernorm(proj[rs, d:2 * d], slg_ref[...], slb_ref[...])
        vn_s[rs, :] = vn.astype(BF16)
    for gi in range(SGU_GROUPS):
        gs = slice(gi * SGU_GROUP_DIM, (gi + 1) * SGU_GROUP_DIM)
        rhs = jnp.concatenate(
            [vn_s[c * CHUNK:(c + 1) * CHUNK, gs] for c in range(n_chunks)], axis=1)
        mixed = _dot(wsp_ref[gi], rhs)
        for c in range(n_chunks):
            m = mixed[:, c * SGU_GROUP_DIM:(c + 1) * SGU_GROUP_DIM] + bsp_ref[:, gs]
            u = proj[c * CHUNK:(c + 1) * CHUNK, gs]
            act[c * CHUNK:(c + 1) * CHUNK, gs] = (u * m).astype(BF16)
    y_b = _dot(act[...], wso_ref[...])

    for rb in range(n_chunks):
        rs = slice(rb * CHUNK, (rb + 1) * CHUNK)
        ga = jax.nn.sigmoid(proj[rs, 2 * d:3 * d] + gbias_ref[:, 0:d])
        gb = jax.nn.sigmoid(proj[rs, 3 * d:4 * d] + gbias_ref[:, d:2 * d])
        act[rs, :] = (ga * y_a[rs, :] + gb * y_b[rs, :]).astype(BF16)
    o_ref[0] = x_ref[0] + _dot(act[...], wo_ref[...])


def _ffn_kernel(x_ref, nrm_ref, wg_ref, wu_ref, wd_ref, fin_ref, o_ref, hid, *, final_norm):
    x = x_ref[...]
    h = _rms(x, nrm_ref[...]).astype(BF16)
    d_ff = wg_ref.shape[1]
    for j in range(d_ff // FFN_COLS):
        cs = slice(j * FFN_COLS, (j + 1) * FFN_COLS)
        gt = _dot(h, wg_ref[:, cs])
        up = _dot(h, wu_ref[:, cs])
        hid[:, cs] = (gt * jax.nn.sigmoid(gt) * up).astype(BF16)
    y = x + _dot(hid[...], wd_ref[...])
    if final_norm:
        y = _rms(y, fin_ref[...])
    o_ref[...] = y


def _resident(shape):
    nd = len(shape)
    return pl.BlockSpec(shape, lambda *_: (0,) * nd, pipeline_mode=pl.Buffered(1))


def _mixer(x, nrm, w_in, gbias, cw, cb, clg, clb, wco, slg, slb, wsp, bsp, wso, wo):
    bsz, seq, d = x.shape
    tile = MIX_TILE
    ext = tile + 2 * HALO
    assert seq % tile == 0 and tile % CHUNK == 0 and tile % HALO == 0
    assert ext % PACKED == 0 and (ext // PACKED) % 8 != 0
    assert CONV_PAD <= C_OFF and CONV_PAD <= HALO
    nt = seq // tile
    hb = tile // HALO
    last_hb = seq // HALO - 1
    n_slabs = d // LANES
    params = (nrm, w_in, gbias, cw, cb, clg, clb, wco, slg, slb, wsp, bsp, wso, wo)
    in_specs = [
        pl.BlockSpec((1, tile, d), lambda b, t: (b, t, 0)),
        pl.BlockSpec((1, HALO, d), lambda b, t: (b, jnp.maximum(t * hb - 1, 0), 0)),
        pl.BlockSpec((1, HALO, d), lambda b, t: (b, jnp.minimum((t + 1) * hb, last_hb), 0)),
    ] + [_resident(p.shape) for p in params]
    return pl.pallas_call(
        _mixer_kernel,
        grid=(bsz, nt),
        in_specs=in_specs,
        out_specs=pl.BlockSpec((1, tile, d), lambda b, t: (b, t, 0)),
        out_shape=jax.ShapeDtypeStruct(x.shape, x.dtype),
        scratch_shapes=[
            pltpu.VMEM((ext, d), BF16),
            pltpu.VMEM((n_slabs, ext + 2 * C_OFF, LANES), F32),
            pltpu.VMEM((n_slabs, (ext // PACKED + 2 * CONV_PAD) * PACKED, LANES), BF16),
            pltpu.VMEM((tile, 4 * d), F32),
            pltpu.VMEM((n_slabs, ext, LANES), F32),
            pltpu.VMEM((tile, d), F32),
            pltpu.VMEM((tile, d), BF16),
            pltpu.VMEM((tile, d), BF16),
        ],
        compiler_params=pltpu.CompilerParams(
            dimension_semantics=("arbitrary", "arbitrary"),
            vmem_limit_bytes=V7X_VMEM_LIMIT_BYTES),
        name="mixer",
    )(x, x, x, *params)


def _ffn(x2, nrm, wg, wu, wd, fin, final_norm):
    n, d = x2.shape
    tile = FFN_TILE
    assert n % tile == 0 and wg.shape[1] % FFN_COLS == 0
    params = (nrm, wg, wu, wd, fin)
    return pl.pallas_call(
        functools.partial(_ffn_kernel, final_norm=final_norm),
        grid=(n // tile,),
        in_specs=[pl.BlockSpec((tile, d), lambda i: (i, 0))] + [_resident(p.shape) for p in params],
        out_specs=pl.BlockSpec((tile, d), lambda i: (i, 0)),
        out_shape=jax.ShapeDtypeStruct(x2.shape, x2.dtype),
        scratch_shapes=[pltpu.VMEM((tile, wg.shape[1]), BF16)],
        compiler_params=pltpu.CompilerParams(
            dimension_semantics=("arbitrary",),
            vmem_limit_bytes=V7X_VMEM_LIMIT_BYTES),
        name="ffn",
    )(x2, *params)


def _mixer_layer(x, l, norm_mix, w_in, gate_bias, conv_w, conv_b, conv_ln_g, conv_ln_b, w_conv_out,
                 sgu_ln_g, sgu_ln_b, w_spatial, b_spatial, w_sgu_out, w_o):
    d = x.shape[-1]
    n_slabs = d // LANES
    row = lambda v: v.reshape(1, -1).astype(F32)
    cw = conv_w[l].reshape(CONV_WIDTH, n_slabs, LANES).transpose(1, 0, 2).astype(BF16)
    cw = jnp.repeat(cw, PACKED, axis=1)
    cb = conv_b[l].reshape(n_slabs, 1, LANES).astype(F32)
    bsp = jnp.repeat(jnp.transpose(b_spatial[l]).astype(F32), SGU_GROUP_DIM, axis=1)
    return _mixer(
        x, row(norm_mix[l]), w_in[l].astype(BF16), row(gate_bias[l]), cw, cb,
        row(conv_ln_g[l]), row(conv_ln_b[l]), w_conv_out[l].astype(BF16),
        row(sgu_ln_g[l]), row(sgu_ln_b[l]), w_spatial[l].astype(BF16), bsp,
        w_sgu_out[l].astype(BF16), w_o[l].astype(BF16))


def kernel(x, norm_mix, w_in, gate_bias, conv_w, conv_b, conv_ln_g, conv_ln_b, w_conv_out, sgu_ln_g, sgu_ln_b, w_spatial, b_spatial, w_sgu_out, w_o, norm_ffn, w_ffn_gate, w_ffn_up, w_ffn_down, norm_final):
    bsz, seq, d = x.shape
    depth = w_in.shape[0]
    row = lambda v: v.reshape(1, -1).astype(F32)
    fin = row(norm_final)
    for l in range(depth):
        x = _mixer_layer(x, l, norm_mix, w_in, gate_bias, conv_w, conv_b, conv_ln_g, conv_ln_b,
                         w_conv_out, sgu_ln_g, sgu_ln_b, w_spatial, b_spatial, w_sgu_out, w_o)
        x = _ffn(
            x.reshape(bsz * seq, d), row(norm_ffn[l]), w_ffn_gate[l].astype(BF16),
            w_ffn_up[l].astype(BF16), w_ffn_down[l].astype(BF16), fin,
            final_norm=(l == depth - 1)).reshape(bsz, seq, d)
    return x
```

```python
import functools

import jax
import jax.numpy as jnp
from jax import lax
from jax.experimental import pallas as pl
from jax.experimental.pallas import tpu as pltpu

D_MODEL = 1024
CONV_WIDTH = 31
CONV_PAD = (CONV_WIDTH - 1) // 2
SGU_GROUPS = 8
SGU_GROUP_DIM = D_MODEL // SGU_GROUPS
CHUNK = 128
EPS = 1e-6

LANES = 128
SUBLANES = 8
PACKED = 16
HALO = 16
C_OFF = 16
MIX_TILE = 512
FFN_TILE = 1024
GLU_COLS = 256
PROJ_COLS = 512
PROJ_EARLY = 2
FFN_COLS = 256
V7X_VMEM_LIMIT_BYTES = 60 * 1024 * 1024

F32 = jnp.float32
BF16 = jnp.bfloat16


def _rms(v, g):
    ms = jnp.mean(v * v, axis=-1, keepdims=True)
    return v * lax.rsqrt(ms + EPS) * g


def _layernorm(v, g, b):
    mu = jnp.mean(v, axis=-1, keepdims=True)
    d = v - mu
    var = jnp.mean(d * d, axis=-1, keepdims=True)
    return d * lax.rsqrt(var + EPS) * g + b


def _dot(a, b):
    return jnp.dot(a, b, preferred_element_type=F32)


def _mixer_kernel(x_ref, xp_ref, xn_ref, nrm_ref, w_in_ref, gbias_ref, cw_ref, cb_ref,
                  clg_ref, clb_ref, wco_ref, slg_ref, slb_ref, wsp_ref, bsp_ref,
                  wso_ref, wo_ref, o_ref,
                  h_ext, c_pad, c_rows, proj, conv_o, y_a, vn_s, act):
    t = pl.program_id(1)
    nt = pl.num_programs(1)
    tile = x_ref.shape[1]
    ext = tile + 2 * HALO
    pitch = (tile + PACKED) // PACKED
    e0 = (ext - pitch * PACKED) // 2
    n_slabs = D_MODEL // LANES
    d = D_MODEL
    g = nrm_ref[...]

    h_ext[0:HALO, :] = _rms(xp_ref[0], g).astype(BF16)
    h_ext[HALO:HALO + tile, :] = _rms(x_ref[0], g).astype(BF16)
    h_ext[HALO + tile:, :] = _rms(xn_ref[0], g).astype(BF16)

    hv = h_ext[...]
    for j in range(d // GLU_COLS):
        val = _dot(hv, w_in_ref[:, j * GLU_COLS:(j + 1) * GLU_COLS])
        gate = _dot(hv, w_in_ref[:, d + j * GLU_COLS:d + (j + 1) * GLU_COLS])
        c = val * jax.nn.sigmoid(gate)
        for i in range(GLU_COLS // LANES):
            c_pad[j * (GLU_COLS // LANES) + i, C_OFF:C_OFF + ext, :] = c[:, i * LANES:(i + 1) * LANES]
    lo = slice(C_OFF, C_OFF + HALO)
    hi = slice(C_OFF + HALO + tile, C_OFF + ext)
    c_pad[:, lo, :] = jnp.where(t > 0, c_pad[:, lo, :], 0.0)
    c_pad[:, hi, :] = jnp.where(t < nt - 1, c_pad[:, hi, :], 0.0)
    c_pad[:, 0:C_OFF, :] = jnp.zeros((n_slabs, C_OFF, LANES), F32)
    c_pad[:, C_OFF + ext:, :] = jnp.zeros((n_slabs, C_OFF, LANES), F32)

    hm = h_ext[HALO:HALO + tile, :]

    def proj_chunk(j):
        cs = slice(j * PROJ_COLS, (j + 1) * PROJ_COLS)
        proj[:, cs] = _dot(hm, w_in_ref[:, 2 * d + j * PROJ_COLS:2 * d + (j + 1) * PROJ_COLS])
    for j in range(PROJ_EARLY):
        proj_chunk(j)

    for s in range(n_slabs):
        for v in range(-CONV_PAD, pitch + CONV_PAD):
            top = c_pad[s, pl.ds(C_OFF + e0 + v, SUBLANES, stride=pitch), :]
            bot = c_pad[s, pl.ds(C_OFF + e0 + v + SUBLANES * pitch, SUBLANES, stride=pitch), :]
            c_rows[s, (v + CONV_PAD) * PACKED:(v + CONV_PAD + 1) * PACKED, :] = (
                jnp.concatenate([top, bot], axis=0).astype(BF16))

    def conv_region(_, carry):
        for s in range(n_slabs):
            bias = jnp.broadcast_to(cb_ref[s], (PACKED, LANES))
            wk = [cw_ref[s, k * PACKED:(k + 1) * PACKED, :] for k in range(CONV_WIDTH)]
            for m in range(pitch):
                acc = jnp.zeros((PACKED, LANES), F32)
                for k in range(CONV_WIDTH):
                    acc = acc + (wk[k].astype(F32)
                                 * c_rows[s, (m + k) * PACKED:(m + k + 1) * PACKED, :].astype(F32))
                acc = acc + bias
                conv_o[s, pl.ds(e0 + m, SUBLANES, stride=pitch), :] = acc[0:SUBLANES]
                conv_o[s, pl.ds(e0 + m + SUBLANES * pitch, SUBLANES, stride=pitch), :] = acc[SUBLANES:]
        return carry
    lax.fori_loop(0, jnp.minimum(t + 1, 1), conv_region, 0)

    for j in range(PROJ_EARLY, 4 * d // PROJ_COLS):
        proj_chunk(j)

    n_chunks = tile // CHUNK
    for rb in range(n_chunks):
        rs = slice(rb * CHUNK, (rb + 1) * CHUNK)
        es = slice(HALO + rb * CHUNK, HALO + (rb + 1) * CHUNK)
        cv = jnp.concatenate([conv_o[s, es, :] for s in range(n_slabs)], axis=1)
        cn = _layernorm(cv, clg_ref[...], clb_ref[...])
        act[rs, :] = (cn * jax.nn.sigmoid(cn)).astype(BF16)
    y_a[...] = _dot(act[...], wco_ref[...])

    for rb in range(n_chunks):
        rs = slice(rb * CHUNK, (rb + 1) * CHUNK)
        vn = _layernorm(proj[rs, d:2 * d], slg_ref[...], slb_ref[...])
        vn_s[rs, :] = vn.astype(BF16)
    for gi in range(SGU_GROUPS):
        gs = slice(gi * SGU_GROUP_DIM, (gi + 1) * SGU_GROUP_DIM)
        rhs = jnp.concatenate(
            [vn_s[c * CHUNK:(c + 1) * CHUNK, gs] for c in range(n_chunks)], axis=1)
        mixed = _dot(wsp_ref[gi], rhs)
        for c in range(n_chunks):
            m = mixed[:, c * SGU_GROUP_DIM:(c + 1) * SGU_GROUP_DIM] + bsp_ref[:, gs]
            u = proj[c * CHUNK:(c + 1) * CHUNK, gs]
            act[c * CHUNK:(c + 1) * CHUNK, gs] = (u * m).astype(BF16)
    y_b = _dot(act[...], wso_ref[...])

    for rb in range(n_chunks):
        rs = slice(rb * CHUNK, (rb + 1) * CHUNK)
        ga = jax.nn.sigmoid(proj[rs, 2 * d:3 * d] + gbias_ref[:, 0:d])
        gb = jax.nn.sigmoid(proj[rs, 3 * d:4 * d] + gbias_ref[:, d:2 * d])
        act[rs, :] = (ga * y_a[rs, :] + gb * y_b[rs, :]).astype(BF16)
    o_ref[0] = x_ref[0] + _dot(act[...], wo_ref[...])


def _ffn_kernel(x_ref, nrm_ref, wg_ref, wu_ref, wd_ref, fin_ref, o_ref, hid, *, final_norm):
    x = x_ref[...]
    h = _rms(x, nrm_ref[...]).astype(BF16)
    d_ff = wg_ref.shape[1]
    for j in range(d_ff // FFN_COLS):
        cs = slice(j * FFN_COLS, (j + 1) * FFN_COLS)
        gt = _dot(h, wg_ref[:, cs])
        up = _dot(h, wu_ref[:, cs])
        hid[:, cs] = (gt * jax.nn.sigmoid(gt) * up).astype(BF16)
    y = x + _dot(hid[...], wd_ref[...])
    if final_norm:
        y = _rms(y, fin_ref[...])
    o_ref[...] = y


def _resident(shape):
    nd = len(shape)
    return pl.BlockSpec(shape, lambda *_: (0,) * nd, pipeline_mode=pl.Buffered(1))


def _mixer(x, nrm, w_in, gbias, cw, cb, clg, clb, wco, slg, slb, wsp, bsp, wso, wo):
    bsz, seq, d = x.shape
    tile = MIX_TILE
    ext = tile + 2 * HALO
    assert seq % tile == 0 and tile % CHUNK == 0 and tile % HALO == 0
    assert ((tile + PACKED) // PACKED) % 8 != 0 and (ext - tile - PACKED) // 2 <= HALO
    assert CONV_PAD <= C_OFF and CONV_PAD <= HALO
    nt = seq // tile
    hb = tile // HALO
    last_hb = seq // HALO - 1
    n_slabs = d // LANES
    params = (nrm, w_in, gbias, cw, cb, clg, clb, wco, slg, slb, wsp, bsp, wso, wo)
    in_specs = [
        pl.BlockSpec((1, tile, d), lambda b, t: (b, t, 0)),
        pl.BlockSpec((1, HALO, d), lambda b, t: (b, jnp.maximum(t * hb - 1, 0), 0)),
        pl.BlockSpec((1, HALO, d), lambda b, t: (b, jnp.minimum((t + 1) * hb, last_hb), 0)),
    ] + [_resident(p.shape) for p in params]
    return pl.pallas_call(
        _mixer_kernel,
        grid=(bsz, nt),
        in_specs=in_specs,
        out_specs=pl.BlockSpec((1, tile, d), lambda b, t: (b, t, 0)),
        out_shape=jax.ShapeDtypeStruct(x.shape, x.dtype),
        scratch_shapes=[
            pltpu.VMEM((ext, d), BF16),
            pltpu.VMEM((n_slabs, ext + 2 * C_OFF, LANES), F32),
            pltpu.VMEM((n_slabs, ((tile + PACKED) // PACKED + 2 * CONV_PAD) * PACKED, LANES), BF16),
            pltpu.VMEM((tile, 4 * d), F32),
            pltpu.VMEM((n_slabs, ext, LANES), F32),
            pltpu.VMEM((tile, d), F32),
            pltpu.VMEM((tile, d), BF16),
            pltpu.VMEM((tile, d), BF16),
        ],
        compiler_params=pltpu.CompilerParams(
            dimension_semantics=("arbitrary", "arbitrary"),
            vmem_limit_bytes=V7X_VMEM_LIMIT_BYTES),
        name="mixer",
    )(x, x, x, *params)


def _ffn(x2, nrm, wg, wu, wd, fin, final_norm):
    n, d = x2.shape
    tile = FFN_TILE
    assert n % tile == 0 and wg.shape[1] % FFN_COLS == 0
    params = (nrm, wg, wu, wd, fin)
    return pl.pallas_call(
        functools.partial(_ffn_kernel, final_norm=final_norm),
        grid=(n // tile,),
        in_specs=[pl.BlockSpec((tile, d), lambda i: (i, 0))] + [_resident(p.shape) for p in params],
        out_specs=pl.BlockSpec((tile, d), lambda i: (i, 0)),
        out_shape=jax.ShapeDtypeStruct(x2.shape, x2.dtype),
        scratch_shapes=[pltpu.VMEM((tile, wg.shape[1]), BF16)],
        compiler_params=pltpu.CompilerParams(
            dimension_semantics=("arbitrary",),
            vmem_limit_bytes=V7X_VMEM_LIMIT_BYTES),
        name="ffn",
    )(x2, *params)


def _mixer_layer(x, l, norm_mix, w_in, gate_bias, conv_w, conv_b, conv_ln_g, conv_ln_b, w_conv_out,
                 sgu_ln_g, sgu_ln_b, w_spatial, b_spatial, w_sgu_out, w_o):
    d = x.shape[-1]
    n_slabs = d // LANES
    row = lambda v: v.reshape(1, -1).astype(F32)
    cw = conv_w[l].reshape(CONV_WIDTH, n_slabs, LANES).transpose(1, 0, 2).astype(BF16)
    cw = jnp.repeat(cw, PACKED, axis=1)
    cb = conv_b[l].reshape(n_slabs, 1, LANES).astype(F32)
    bsp = jnp.repeat(jnp.transpose(b_spatial[l]).astype(F32), SGU_GROUP_DIM, axis=1)
    return _mixer(
        x, row(norm_mix[l]), w_in[l].astype(BF16), row(gate_bias[l]), cw, cb,
        row(conv_ln_g[l]), row(conv_ln_b[l]), w_conv_out[l].astype(BF16),
        row(sgu_ln_g[l]), row(sgu_ln_b[l]), w_spatial[l].astype(BF16), bsp,
        w_sgu_out[l].astype(BF16), w_o[l].astype(BF16))


def kernel(x, norm_mix, w_in, gate_bias, conv_w, conv_b, conv_ln_g, conv_ln_b, w_conv_out, sgu_ln_g, sgu_ln_b, w_spatial, b_spatial, w_sgu_out, w_o, norm_ffn, w_ffn_gate, w_ffn_up, w_ffn_down, norm_final):
    bsz, seq, d = x.shape
    depth = w_in.shape[0]
    row = lambda v: v.reshape(1, -1).astype(F32)
    fin = row(norm_final)
    for l in range(depth):
        x = _mixer_layer(x, l, norm_mix, w_in, gate_bias, conv_w, conv_b, conv_ln_g, conv_ln_b,
                         w_conv_out, sgu_ln_g, sgu_ln_b, w_spatial, b_spatial, w_sgu_out, w_o)
        x = _ffn(
            x.reshape(bsz * seq, d), row(norm_ffn[l]), w_ffn_gate[l].astype(BF16),
            w_ffn_up[l].astype(BF16), w_ffn_down[l].astype(BF16), fin,
            final_norm=(l == depth - 1)).reshape(bsz, seq, d)
    return x
```

```python
import functools

import jax
import jax.numpy as jnp
from jax import lax
from jax.experimental import pallas as pl
from jax.experimental.pallas import tpu as pltpu

D_MODEL = 1024
CONV_WIDTH = 31
CONV_PAD = (CONV_WIDTH - 1) // 2
SGU_GROUPS = 8
SGU_GROUP_DIM = D_MODEL // SGU_GROUPS
CHUNK = 128
EPS = 1e-6

LANES = 128
SUBLANES = 8
PACKED = 16
HALO = 16
C_OFF = 16
MIX_TILE = 512
FFN_TILE = 1024
GLU_COLS = 256
PROJ_COLS = 512
PROJ_EARLY = 2
FFN_COLS = 256
V7X_VMEM_LIMIT_BYTES = 60 * 1024 * 1024

F32 = jnp.float32
BF16 = jnp.bfloat16


def _rms(v, g):
    ms = jnp.mean(v * v, axis=-1, keepdims=True)
    return v * lax.rsqrt(ms + EPS) * g


def _layernorm(v, g, b):
    mu = jnp.mean(v, axis=-1, keepdims=True)
    d = v - mu
    var = jnp.mean(d * d, axis=-1, keepdims=True)
    return d * lax.rsqrt(var + EPS) * g + b


def _dot(a, b):
    return jnp.dot(a, b, preferred_element_type=F32)


def _mixer_kernel(x_ref, xn_ref, nrm_ref, w_in_ref, gbias_ref, cw_ref, cb_ref,
                  clg_ref, clb_ref, wco_ref, slg_ref, slb_ref, wsp_ref, bsp_ref,
                  wso_ref, wo_ref, o_ref,
                  h_ext, c_pad, c_rows, proj, conv_o, y_a, vn_s, act):
    t = pl.program_id(1)
    nt = pl.num_programs(1)
    tile = x_ref.shape[1]
    ext = tile + 2 * HALO
    pitch = (tile + PACKED) // PACKED
    e0 = (ext - pitch * PACKED) // 2
    n_slabs = D_MODEL // LANES
    d = D_MODEL
    g = nrm_ref[...]

    lo = slice(C_OFF, C_OFF + HALO)
    hi = slice(C_OFF + HALO + tile, C_OFF + ext)

    @pl.when(t > 0)
    def _():
        c_pad[:, lo, :] = c_pad[:, C_OFF + tile:C_OFF + tile + HALO, :]

    @pl.when(t == 0)
    def _():
        c_pad[:, lo, :] = jnp.zeros((n_slabs, HALO, LANES), F32)

    h_ext[0:tile, :] = _rms(x_ref[0], g).astype(BF16)
    h_ext[tile:, :] = _rms(xn_ref[0], g).astype(BF16)

    hv = h_ext[...]
    for j in range(d // GLU_COLS):
        val = _dot(hv, w_in_ref[:, j * GLU_COLS:(j + 1) * GLU_COLS])
        gate = _dot(hv, w_in_ref[:, d + j * GLU_COLS:d + (j + 1) * GLU_COLS])
        c = val * jax.nn.sigmoid(gate)
        for i in range(GLU_COLS // LANES):
            c_pad[j * (GLU_COLS // LANES) + i, C_OFF + HALO:C_OFF + ext, :] = c[:, i * LANES:(i + 1) * LANES]
    c_pad[:, hi, :] = jnp.where(t < nt - 1, c_pad[:, hi, :], 0.0)
    c_pad[:, 0:C_OFF, :] = jnp.zeros((n_slabs, C_OFF, LANES), F32)
    c_pad[:, C_OFF + ext:, :] = jnp.zeros((n_slabs, C_OFF, LANES), F32)

    hm = h_ext[0:tile, :]

    def proj_chunk(j):
        cs = slice(j * PROJ_COLS, (j + 1) * PROJ_COLS)
        proj[:, cs] = _dot(hm, w_in_ref[:, 2 * d + j * PROJ_COLS:2 * d + (j + 1) * PROJ_COLS])
    for j in range(PROJ_EARLY):
        proj_chunk(j)

    for s in range(n_slabs):
        for v in range(-CONV_PAD, pitch + CONV_PAD):
            top = c_pad[s, pl.ds(C_OFF + e0 + v, SUBLANES, stride=pitch), :]
            bot = c_pad[s, pl.ds(C_OFF + e0 + v + SUBLANES * pitch, SUBLANES, stride=pitch), :]
            c_rows[s, (v + CONV_PAD) * PACKED:(v + CONV_PAD + 1) * PACKED, :] = (
                jnp.concatenate([top, bot], axis=0).astype(BF16))

    def conv_region(_, carry):
        for s in range(n_slabs):
            bias = jnp.broadcast_to(cb_ref[s], (PACKED, LANES))
            wk = [cw_ref[s, k * PACKED:(k + 1) * PACKED, :] for k in range(CONV_WIDTH)]
            for m in range(pitch):
                acc = jnp.zeros((PACKED, LANES), F32)
                for k in range(CONV_WIDTH):
                    acc = acc + (wk[k].astype(F32)
                                 * c_rows[s, (m + k) * PACKED:(m + k + 1) * PACKED, :].astype(F32))
                acc = acc + bias
                conv_o[s, pl.ds(e0 + m, SUBLANES, stride=pitch), :] = acc[0:SUBLANES]
                conv_o[s, pl.ds(e0 + m + SUBLANES * pitch, SUBLANES, stride=pitch), :] = acc[SUBLANES:]
        return carry
    lax.fori_loop(0, jnp.minimum(t + 1, 1), conv_region, 0)

    for j in range(PROJ_EARLY, 4 * d // PROJ_COLS):
        proj_chunk(j)

    n_chunks = tile // CHUNK
    for rb in range(n_chunks):
        rs = slice(rb * CHUNK, (rb + 1) * CHUNK)
        es = slice(HALO + rb * CHUNK, HALO + (rb + 1) * CHUNK)
        cv = jnp.concatenate([conv_o[s, es, :] for s in range(n_slabs)], axis=1)
        cn = _layernorm(cv, clg_ref[...], clb_ref[...])
        act[rs, :] = (cn * jax.nn.sigmoid(cn)).astype(BF16)
    y_a[...] = _dot(act[...], wco_ref[...])

    for rb in range(n_chunks):
        rs = slice(rb * CHUNK, (rb + 1) * CHUNK)
        vn = _layernorm(proj[rs, d:2 * d], slg_ref[...], slb_ref[...])
        vn_s[rs, :] = vn.astype(BF16)
    for gi in range(SGU_GROUPS):
        gs = slice(gi * SGU_GROUP_DIM, (gi + 1) * SGU_GROUP_DIM)
        rhs = jnp.concatenate(
            [vn_s[c * CHUNK:(c + 1) * CHUNK, gs] for c in range(n_chunks)], axis=1)
        mixed = _dot(wsp_ref[gi], rhs)
        for c in range(n_chunks):
            m = mixed[:, c * SGU_GROUP_DIM:(c + 1) * SGU_GROUP_DIM] + bsp_ref[:, gs]
            u = proj[c * CHUNK:(c + 1) * CHUNK, gs]
            act[c * CHUNK:(c + 1) * CHUNK, gs] = (u * m).astype(BF16)
    y_b = _dot(act[...], wso_ref[...])

    for rb in range(n_chunks):
        rs = slice(rb * CHUNK, (rb + 1) * CHUNK)
        ga = jax.nn.sigmoid(proj[rs, 2 * d:3 * d] + gbias_ref[:, 0:d])
        gb = jax.nn.sigmoid(proj[rs, 3 * d:4 * d] + gbias_ref[:, d:2 * d])
        act[rs, :] = (ga * y_a[rs, :] + gb * y_b[rs, :]).astype(BF16)
    o_ref[0] = x_ref[0] + _dot(act[...], wo_ref[...])


def _ffn_kernel(x_ref, nrm_ref, wg_ref, wu_ref, wd_ref, fin_ref, o_ref, hid, *, final_norm):
    x = x_ref[...]
    h = _rms(x, nrm_ref[...]).astype(BF16)
    d_ff = wg_ref.shape[1]
    for j in range(d_ff // FFN_COLS):
        cs = slice(j * FFN_COLS, (j + 1) * FFN_COLS)
        gt = _dot(h, wg_ref[:, cs])
        up = _dot(h, wu_ref[:, cs])
        hid[:, cs] = (gt * jax.nn.sigmoid(gt) * up).astype(BF16)
    y = x + _dot(hid[...], wd_ref[...])
    if final_norm:
        y = _rms(y, fin_ref[...])
    o_ref[...] = y


def _resident(shape):
    nd = len(shape)
    return pl.BlockSpec(shape, lambda *_: (0,) * nd, pipeline_mode=pl.Buffered(1))


def _mixer(x, nrm, w_in, gbias, cw, cb, clg, clb, wco, slg, slb, wsp, bsp, wso, wo):
    bsz, seq, d = x.shape
    tile = MIX_TILE
    ext = tile + 2 * HALO
    assert seq % tile == 0 and tile % CHUNK == 0 and tile % HALO == 0
    assert ((tile + PACKED) // PACKED) % 8 != 0 and (ext - tile - PACKED) // 2 <= HALO
    assert CONV_PAD <= C_OFF and CONV_PAD <= HALO
    nt = seq // tile
    hb = tile // HALO
    last_hb = seq // HALO - 1
    n_slabs = d // LANES
    params = (nrm, w_in, gbias, cw, cb, clg, clb, wco, slg, slb, wsp, bsp, wso, wo)
    in_specs = [
        pl.BlockSpec((1, tile, d), lambda b, t: (b, t, 0)),
        pl.BlockSpec((1, HALO, d), lambda b, t: (b, jnp.minimum((t + 1) * hb, last_hb), 0)),
    ] + [_resident(p.shape) for p in params]
    return pl.pallas_call(
        _mixer_kernel,
        grid=(bsz, nt),
        in_specs=in_specs,
        out_specs=pl.BlockSpec((1, tile, d), lambda b, t: (b, t, 0)),
        out_shape=jax.ShapeDtypeStruct(x.shape, x.dtype),
        scratch_shapes=[
            pltpu.VMEM((tile + HALO, d), BF16),
            pltpu.VMEM((n_slabs, ext + 2 * C_OFF, LANES), F32),
            pltpu.VMEM((n_slabs, ((tile + PACKED) // PACKED + 2 * CONV_PAD) * PACKED, LANES), BF16),
            pltpu.VMEM((tile, 4 * d), F32),
            pltpu.VMEM((n_slabs, ext, LANES), F32),
            pltpu.VMEM((tile, d), F32),
            pltpu.VMEM((tile, d), BF16),
            pltpu.VMEM((tile, d), BF16),
        ],
        compiler_params=pltpu.CompilerParams(
            dimension_semantics=("arbitrary", "arbitrary"),
            vmem_limit_bytes=V7X_VMEM_LIMIT_BYTES),
        name="mixer",
    )(x, x, *params)


def _ffn(x2, nrm, wg, wu, wd, fin, final_norm):
    n, d = x2.shape
    tile = FFN_TILE
    assert n % tile == 0 and wg.shape[1] % FFN_COLS == 0
    params = (nrm, wg, wu, wd, fin)
    return pl.pallas_call(
        functools.partial(_ffn_kernel, final_norm=final_norm),
        grid=(n // tile,),
        in_specs=[pl.BlockSpec((tile, d), lambda i: (i, 0))] + [_resident(p.shape) for p in params],
        out_specs=pl.BlockSpec((tile, d), lambda i: (i, 0)),
        out_shape=jax.ShapeDtypeStruct(x2.shape, x2.dtype),
        scratch_shapes=[pltpu.VMEM((tile, wg.shape[1]), BF16)],
        compiler_params=pltpu.CompilerParams(
            dimension_semantics=("arbitrary",),
            vmem_limit_bytes=V7X_VMEM_LIMIT_BYTES),
        name="ffn",
    )(x2, *params)


def _mixer_layer(x, l, norm_mix, w_in, gate_bias, conv_w, conv_b, conv_ln_g, conv_ln_b, w_conv_out,
                 sgu_ln_g, sgu_ln_b, w_spatial, b_spatial, w_sgu_out, w_o):
    d = x.shape[-1]
    n_slabs = d // LANES
    row = lambda v: v.reshape(1, -1).astype(F32)
    cw = conv_w[l].reshape(CONV_WIDTH, n_slabs, LANES).transpose(1, 0, 2).astype(BF16)
    cw = jnp.repeat(cw, PACKED, axis=1)
    cb = conv_b[l].reshape(n_slabs, 1, LANES).astype(F32)
    bsp = jnp.repeat(jnp.transpose(b_spatial[l]).astype(F32), SGU_GROUP_DIM, axis=1)
    return _mixer(
        x, row(norm_mix[l]), w_in[l].astype(BF16), row(gate_bias[l]), cw, cb,
        row(conv_ln_g[l]), row(conv_ln_b[l]), w_conv_out[l].astype(BF16),
        row(sgu_ln_g[l]), row(sgu_ln_b[l]), w_spatial[l].astype(BF16), bsp,
        w_sgu_out[l].astype(BF16), w_o[l].astype(BF16))


def kernel(x, norm_mix, w_in, gate_bias, conv_w, conv_b, conv_ln_g, conv_ln_b, w_conv_out, sgu_ln_g, sgu_ln_b, w_spatial, b_spatial, w_sgu_out, w_o, norm_ffn, w_ffn_gate, w_ffn_up, w_ffn_down, norm_final):
    bsz, seq, d = x.shape
    depth = w_in.shape[0]
    row = lambda v: v.reshape(1, -1).astype(F32)
    fin = row(norm_final)
    for l in range(depth):
        x = _mixer_layer(x, l, norm_mix, w_in, gate_bias, conv_w, conv_b, conv_ln_g, conv_ln_b,
                         w_conv_out, sgu_ln_g, sgu_ln_b, w_spatial, b_spatial, w_sgu_out, w_o)
        x = _ffn(
            x.reshape(bsz * seq, d), row(norm_ffn[l]), w_ffn_gate[l].astype(BF16),
            w_ffn_up[l].astype(BF16), w_ffn_down[l].astype(BF16), fin,
            final_norm=(l == depth - 1)).reshape(bsz, seq, d)
    return x
```

```python
import functools

import jax
import jax.numpy as jnp
from jax import lax
from jax.experimental import pallas as pl
from jax.experimental.pallas import tpu as pltpu

D_MODEL = 1024
CONV_WIDTH = 31
CONV_PAD = (CONV_WIDTH - 1) // 2
SGU_GROUPS = 8
SGU_GROUP_DIM = D_MODEL // SGU_GROUPS
CHUNK = 128
EPS = 1e-6

LANES = 128
SUBLANES = 8
PACKED = 16
HALO = 16
C_OFF = 16
MIX_TILE = 512
FFN_TILE = 1024
GLU_COLS = 256
PROJ_COLS = 512
PROJ_EARLY = 2
FFN_COLS = 256
V7X_VMEM_LIMIT_BYTES = 60 * 1024 * 1024

F32 = jnp.float32
BF16 = jnp.bfloat16


def _rms(v, g):
    ms = jnp.mean(v * v, axis=-1, keepdims=True)
    return v * lax.rsqrt(ms + EPS) * g


def _layernorm(v, g, b):
    mu = jnp.mean(v, axis=-1, keepdims=True)
    d = v - mu
    var = jnp.mean(d * d, axis=-1, keepdims=True)
    return d * lax.rsqrt(var + EPS) * g + b


def _sigmoid(v):
    return 0.5 * jnp.tanh(0.5 * v) + 0.5


def _dot(a, b):
    return jnp.dot(a, b, preferred_element_type=F32)


def _mixer_kernel(x_ref, xp_ref, xn_ref, nrm_ref, w_in_ref, gbias_ref, cw_ref, cb_ref,
                  clg_ref, clb_ref, wco_ref, slg_ref, slb_ref, wsp_ref, bsp_ref,
                  wso_ref, wo_ref, o_ref,
                  h_ext, c_pad, c_rows, proj, conv_o, y_a, vn_s, act):
    t = pl.program_id(1)
    nt = pl.num_programs(1)
    tile = x_ref.shape[1]
    ext = tile + 2 * HALO
    pitch = (tile + PACKED) // PACKED
    e0 = (ext - pitch * PACKED) // 2
    n_slabs = D_MODEL // LANES
    d = D_MODEL
    g = nrm_ref[...]

    h_ext[0:HALO, :] = _rms(xp_ref[0], g).astype(BF16)
    h_ext[HALO:HALO + tile, :] = _rms(x_ref[0], g).astype(BF16)
    h_ext[HALO + tile:, :] = _rms(xn_ref[0], g).astype(BF16)

    hv = h_ext[...]
    for j in range(d // GLU_COLS):
        val = _dot(hv, w_in_ref[:, j * GLU_COLS:(j + 1) * GLU_COLS])
        gate = _dot(hv, w_in_ref[:, d + j * GLU_COLS:d + (j + 1) * GLU_COLS])
        c = val * _sigmoid(gate)
        for i in range(GLU_COLS // LANES):
            c_pad[j * (GLU_COLS // LANES) + i, C_OFF:C_OFF + ext, :] = c[:, i * LANES:(i + 1) * LANES]
    lo = slice(C_OFF, C_OFF + HALO)
    hi = slice(C_OFF + HALO + tile, C_OFF + ext)
    c_pad[:, lo, :] = jnp.where(t > 0, c_pad[:, lo, :], 0.0)
    c_pad[:, hi, :] = jnp.where(t < nt - 1, c_pad[:, hi, :], 0.0)
    c_pad[:, 0:C_OFF, :] = jnp.zeros((n_slabs, C_OFF, LANES), F32)
    c_pad[:, C_OFF + ext:, :] = jnp.zeros((n_slabs, C_OFF, LANES), F32)

    hm = h_ext[HALO:HALO + tile, :]

    def proj_chunk(j):
        cs = slice(j * PROJ_COLS, (j + 1) * PROJ_COLS)
        proj[:, cs] = _dot(hm, w_in_ref[:, 2 * d + j * PROJ_COLS:2 * d + (j + 1) * PROJ_COLS])
    for j in range(PROJ_EARLY):
        proj_chunk(j)

    for s in range(n_slabs):
        for v in range(-CONV_PAD, pitch + CONV_PAD):
            top = c_pad[s, pl.ds(C_OFF + e0 + v, SUBLANES, stride=pitch), :]
            bot = c_pad[s, pl.ds(C_OFF + e0 + v + SUBLANES * pitch, SUBLANES, stride=pitch), :]
            c_rows[s, (v + CONV_PAD) * PACKED:(v + CONV_PAD + 1) * PACKED, :] = (
                jnp.concatenate([top, bot], axis=0).astype(BF16))

    def conv_region(_, carry):
        for s in range(n_slabs):
            bias = jnp.broadcast_to(cb_ref[s], (PACKED, LANES))
            wk = [cw_ref[s, k * PACKED:(k + 1) * PACKED, :] for k in range(CONV_WIDTH)]
            for m in range(pitch):
                acc = jnp.zeros((PACKED, LANES), F32)
                for k in range(CONV_WIDTH):
                    acc = acc + (wk[k].astype(F32)
                                 * c_rows[s, (m + k) * PACKED:(m + k + 1) * PACKED, :].astype(F32))
                acc = acc + bias
                conv_o[s, pl.ds(e0 + m, SUBLANES, stride=pitch), :] = acc[0:SUBLANES]
                conv_o[s, pl.ds(e0 + m + SUBLANES * pitch, SUBLANES, stride=pitch), :] = acc[SUBLANES:]
        return carry
    lax.fori_loop(0, jnp.minimum(t + 1, 1), conv_region, 0)

    for j in range(PROJ_EARLY, 4 * d // PROJ_COLS):
        proj_chunk(j)

    n_chunks = tile // CHUNK
    for rb in range(n_chunks):
        rs = slice(rb * CHUNK, (rb + 1) * CHUNK)
        es = slice(HALO + rb * CHUNK, HALO + (rb + 1) * CHUNK)
        cv = jnp.concatenate([conv_o[s, es, :] for s in range(n_slabs)], axis=1)
        cn = _layernorm(cv, clg_ref[...], clb_ref[...])
        act[rs, :] = (cn * _sigmoid(cn)).astype(BF16)
    y_a[...] = _dot(act[...], wco_ref[...])

    for rb in range(n_chunks):
        rs = slice(rb * CHUNK, (rb + 1) * CHUNK)
        vn = _layernorm(proj[rs, d:2 * d], slg_ref[...], slb_ref[...])
        vn_s[rs, :] = vn.astype(BF16)
    for gi in range(SGU_GROUPS):
        gs = slice(gi * SGU_GROUP_DIM, (gi + 1) * SGU_GROUP_DIM)
        rhs = jnp.concatenate(
            [vn_s[c * CHUNK:(c + 1) * CHUNK, gs] for c in range(n_chunks)], axis=1)
        mixed = _dot(wsp_ref[gi], rhs)
        for c in range(n_chunks):
            m = mixed[:, c * SGU_GROUP_DIM:(c + 1) * SGU_GROUP_DIM] + bsp_ref[:, gs]
            u = proj[c * CHUNK:(c + 1) * CHUNK, gs]
            act[c * CHUNK:(c + 1) * CHUNK, gs] = (u * m).astype(BF16)
    y_b = _dot(act[...], wso_ref[...])

    for rb in range(n_chunks):
        rs = slice(rb * CHUNK, (rb + 1) * CHUNK)
        ga = _sigmoid(proj[rs, 2 * d:3 * d] + gbias_ref[:, 0:d])
        gb = _sigmoid(proj[rs, 3 * d:4 * d] + gbias_ref[:, d:2 * d])
        act[rs, :] = (ga * y_a[rs, :] + gb * y_b[rs, :]).astype(BF16)
    o_ref[0] = x_ref[0] + _dot(act[...], wo_ref[...])


def _ffn_kernel(x_ref, nrm_ref, wg_ref, wu_ref, wd_ref, fin_ref, o_ref, hid, *, final_norm):
    x = x_ref[...]
    h = _rms(x, nrm_ref[...]).astype(BF16)
    d_ff = wg_ref.shape[1]
    for j in range(d_ff // FFN_COLS):
        cs = slice(j * FFN_COLS, (j + 1) * FFN_COLS)
        gt = _dot(h, wg_ref[:, cs])
        up = _dot(h, wu_ref[:, cs])
        hid[:, cs] = (gt * _sigmoid(gt) * up).astype(BF16)
    y = x + _dot(hid[...], wd_ref[...])
    if final_norm:
        y = _rms(y, fin_ref[...])
    o_ref[...] = y


def _resident(shape):
    nd = len(shape)
    return pl.BlockSpec(shape, lambda *_: (0,) * nd, pipeline_mode=pl.Buffered(1))


def _mixer(x, nrm, w_in, gbias, cw, cb, clg, clb, wco, slg, slb, wsp, bsp, wso, wo):
    bsz, seq, d = x.shape
    tile = MIX_TILE
    ext = tile + 2 * HALO
    assert seq % tile == 0 and tile % CHUNK == 0 and tile % HALO == 0
    assert ((tile + PACKED) // PACKED) % 8 != 0 and (ext - tile - PACKED) // 2 <= HALO
    assert CONV_PAD <= C_OFF and CONV_PAD <= HALO
    nt = seq // tile
    hb = tile // HALO
    last_hb = seq // HALO - 1
    n_slabs = d // LANES
    params = (nrm, w_in, gbias, cw, cb, clg, clb, wco, slg, slb, wsp, bsp, wso, wo)
    in_specs = [
        pl.BlockSpec((1, tile, d), lambda b, t: (b, t, 0)),
        pl.BlockSpec((1, HALO, d), lambda b, t: (b, jnp.maximum(t * hb - 1, 0), 0)),
        pl.BlockSpec((1, HALO, d), lambda b, t: (b, jnp.minimum((t + 1) * hb, last_hb), 0)),
    ] + [_resident(p.shape) for p in params]
    return pl.pallas_call(
        _mixer_kernel,
        grid=(bsz, nt),
        in_specs=in_specs,
        out_specs=pl.BlockSpec((1, tile, d), lambda b, t: (b, t, 0)),
        out_shape=jax.ShapeDtypeStruct(x.shape, x.dtype),
        scratch_shapes=[
            pltpu.VMEM((ext, d), BF16),
            pltpu.VMEM((n_slabs, ext + 2 * C_OFF, LANES), F32),
            pltpu.VMEM((n_slabs, ((tile + PACKED) // PACKED + 2 * CONV_PAD) * PACKED, LANES), BF16),
            pltpu.VMEM((tile, 4 * d), F32),
            pltpu.VMEM((n_slabs, ext, LANES), F32),
            pltpu.VMEM((tile, d), F32),
            pltpu.VMEM((tile, d), BF16),
            pltpu.VMEM((tile, d), BF16),
        ],
        compiler_params=pltpu.CompilerParams(
            dimension_semantics=("arbitrary", "arbitrary"),
            vmem_limit_bytes=V7X_VMEM_LIMIT_BYTES),
        name="mixer",
    )(x, x, x, *params)


def _ffn(x2, nrm, wg, wu, wd, fin, final_norm):
    n, d = x2.shape
    tile = FFN_TILE
    assert n % tile == 0 and wg.shape[1] % FFN_COLS == 0
    params = (nrm, wg, wu, wd, fin)
    return pl.pallas_call(
        functools.partial(_ffn_kernel, final_norm=final_norm),
        grid=(n // tile,),
        in_specs=[pl.BlockSpec((tile, d), lambda i: (i, 0))] + [_resident(p.shape) for p in params],
        out_specs=pl.BlockSpec((tile, d), lambda i: (i, 0)),
        out_shape=jax.ShapeDtypeStruct(x2.shape, x2.dtype),
        scratch_shapes=[pltpu.VMEM((tile, wg.shape[1]), BF16)],
        compiler_params=pltpu.CompilerParams(
            dimension_semantics=("arbitrary",),
            vmem_limit_bytes=V7X_VMEM_LIMIT_BYTES),
        name="ffn",
    )(x2, *params)


def _mixer_layer(x, l, norm_mix, w_in, gate_bias, conv_w, conv_b, conv_ln_g, conv_ln_b, w_conv_out,
                 sgu_ln_g, sgu_ln_b, w_spatial, b_spatial, w_sgu_out, w_o):
    d = x.shape[-1]
    n_slabs = d // LANES
    row = lambda v: v.reshape(1, -1).astype(F32)
    cw = conv_w[l].reshape(CONV_WIDTH, n_slabs, LANES).transpose(1, 0, 2).astype(BF16)
    cw = jnp.repeat(cw, PACKED, axis=1)
    cb = conv_b[l].reshape(n_slabs, 1, LANES).astype(F32)
    bsp = jnp.repeat(jnp.transpose(b_spatial[l]).astype(F32), SGU_GROUP_DIM, axis=1)
    return _mixer(
        x, row(norm_mix[l]), w_in[l].astype(BF16), row(gate_bias[l]), cw, cb,
        row(conv_ln_g[l]), row(conv_ln_b[l]), w_conv_out[l].astype(BF16),
        row(sgu_ln_g[l]), row(sgu_ln_b[l]), w_spatial[l].astype(BF16), bsp,
        w_sgu_out[l].astype(BF16), w_o[l].astype(BF16))


def kernel(x, norm_mix, w_in, gate_bias, conv_w, conv_b, conv_ln_g, conv_ln_b, w_conv_out, sgu_ln_g, sgu_ln_b, w_spatial, b_spatial, w_sgu_out, w_o, norm_ffn, w_ffn_gate, w_ffn_up, w_ffn_down, norm_final):
    bsz, seq, d = x.shape
    depth = w_in.shape[0]
    row = lambda v: v.reshape(1, -1).astype(F32)
    fin = row(norm_final)
    for l in range(depth):
        x = _mixer_layer(x, l, norm_mix, w_in, gate_bias, conv_w, conv_b, conv_ln_g, conv_ln_b,
                         w_conv_out, sgu_ln_g, sgu_ln_b, w_spatial, b_spatial, w_sgu_out, w_o)
        x = _ffn(
            x.reshape(bsz * seq, d), row(norm_ffn[l]), w_ffn_gate[l].astype(BF16),
            w_ffn_up[l].astype(BF16), w_ffn_down[l].astype(BF16), fin,
            final_norm=(l == depth - 1)).reshape(bsz, seq, d)
    return x
```

```python
import functools

import jax
import jax.numpy as jnp
from jax import lax
from jax.experimental import pallas as pl
from jax.experimental.pallas import tpu as pltpu

D_MODEL = 1024
CONV_WIDTH = 31
CONV_PAD = (CONV_WIDTH - 1) // 2
SGU_GROUPS = 8
SGU_GROUP_DIM = D_MODEL // SGU_GROUPS
CHUNK = 128
EPS = 1e-6

LANES = 128
SUBLANES = 8
PACKED = 16
HALO = 16
C_OFF = 16
MIX_TILE = 512
FFN_TILE = 1024
GLU_COLS = 256
PROJ_COLS = 512
PROJ_EARLY = 2
FFN_COLS = 256
V7X_VMEM_LIMIT_BYTES = 60 * 1024 * 1024

F32 = jnp.float32
BF16 = jnp.bfloat16


def _rms(v, g):
    ms = jnp.mean(v * v, axis=-1, keepdims=True)
    return v * lax.rsqrt(ms + EPS) * g


def _layernorm(v, g, b):
    mu = jnp.mean(v, axis=-1, keepdims=True)
    d = v - mu
    var = jnp.mean(d * d, axis=-1, keepdims=True)
    return d * lax.rsqrt(var + EPS) * g + b


def _sigmoid(v):
    return 0.5 * jnp.tanh(0.5 * v) + 0.5


def _silu(v):
    h = 0.5 * v
    return h + h * jnp.tanh(h)


def _dot(a, b):
    return jnp.dot(a, b, preferred_element_type=F32)


def _mixer_kernel(x_ref, xp_ref, xn_ref, nrm_ref, w_in_ref, gbias_ref, cw_ref, cb_ref,
                  clg_ref, clb_ref, wco_ref, slg_ref, slb_ref, wsp_ref, bsp_ref,
                  wso_ref, wo_ref, o_ref,
                  h_ext, c_pad, c_rows, proj, conv_o, y_a, vn_s, act):
    t = pl.program_id(1)
    nt = pl.num_programs(1)
    tile = x_ref.shape[1]
    ext = tile + 2 * HALO
    pitch = (tile + PACKED) // PACKED
    e0 = (ext - pitch * PACKED) // 2
    n_slabs = D_MODEL // LANES
    d = D_MODEL
    g = nrm_ref[...]

    h_ext[0:HALO, :] = _rms(xp_ref[0], g).astype(BF16)
    h_ext[HALO:HALO + tile, :] = _rms(x_ref[0], g).astype(BF16)
    h_ext[HALO + tile:, :] = _rms(xn_ref[0], g).astype(BF16)

    hv = h_ext[...]
    for j in range(d // GLU_COLS):
        val = _dot(hv, w_in_ref[:, j * GLU_COLS:(j + 1) * GLU_COLS])
        gate = _dot(hv, w_in_ref[:, d + j * GLU_COLS:d + (j + 1) * GLU_COLS])
        c = val * _sigmoid(gate)
        for i in range(GLU_COLS // LANES):
            c_pad[j * (GLU_COLS // LANES) + i, C_OFF:C_OFF + ext, :] = c[:, i * LANES:(i + 1) * LANES]
    lo = slice(C_OFF, C_OFF + HALO)
    hi = slice(C_OFF + HALO + tile, C_OFF + ext)
    c_pad[:, lo, :] = jnp.where(t > 0, c_pad[:, lo, :], 0.0)
    c_pad[:, hi, :] = jnp.where(t < nt - 1, c_pad[:, hi, :], 0.0)
    c_pad[:, 0:C_OFF, :] = jnp.zeros((n_slabs, C_OFF, LANES), F32)
    c_pad[:, C_OFF + ext:, :] = jnp.zeros((n_slabs, C_OFF, LANES), F32)

    hm = h_ext[HALO:HALO + tile, :]

    def proj_chunk(j):
        cs = slice(j * PROJ_COLS, (j + 1) * PROJ_COLS)
        proj[:, cs] = _dot(hm, w_in_ref[:, 2 * d + j * PROJ_COLS:2 * d + (j + 1) * PROJ_COLS])
    for j in range(PROJ_EARLY):
        proj_chunk(j)

    for s in range(n_slabs):
        for v in range(-CONV_PAD, pitch + CONV_PAD):
            top = c_pad[s, pl.ds(C_OFF + e0 + v, SUBLANES, stride=pitch), :]
            bot = c_pad[s, pl.ds(C_OFF + e0 + v + SUBLANES * pitch, SUBLANES, stride=pitch), :]
            c_rows[s, (v + CONV_PAD) * PACKED:(v + CONV_PAD + 1) * PACKED, :] = (
                jnp.concatenate([top, bot], axis=0).astype(BF16))

    def conv_region(_, carry):
        for s in range(n_slabs):
            bias = jnp.broadcast_to(cb_ref[s], (PACKED, LANES))
            wk = [cw_ref[s, k * PACKED:(k + 1) * PACKED, :] for k in range(CONV_WIDTH)]
            for m in range(pitch):
                acc = jnp.zeros((PACKED, LANES), F32)
                for k in range(CONV_WIDTH):
                    acc = acc + (wk[k].astype(F32)
                                 * c_rows[s, (m + k) * PACKED:(m + k + 1) * PACKED, :].astype(F32))
                acc = acc + bias
                conv_o[s, pl.ds(e0 + m, SUBLANES, stride=pitch), :] = acc[0:SUBLANES]
                conv_o[s, pl.ds(e0 + m + SUBLANES * pitch, SUBLANES, stride=pitch), :] = acc[SUBLANES:]
        return carry
    lax.fori_loop(0, jnp.minimum(t + 1, 1), conv_region, 0)

    for j in range(PROJ_EARLY, 4 * d // PROJ_COLS):
        proj_chunk(j)

    n_chunks = tile // CHUNK
    for rb in range(n_chunks):
        rs = slice(rb * CHUNK, (rb + 1) * CHUNK)
        es = slice(HALO + rb * CHUNK, HALO + (rb + 1) * CHUNK)
        cv = jnp.concatenate([conv_o[s, es, :] for s in range(n_slabs)], axis=1)
        cn = _layernorm(cv, clg_ref[...], clb_ref[...])
        act[rs, :] = _silu(cn).astype(BF16)
    y_a[...] = _dot(act[...], wco_ref[...])

    for rb in range(n_chunks):
        rs = slice(rb * CHUNK, (rb + 1) * CHUNK)
        vn = _layernorm(proj[rs, d:2 * d], slg_ref[...], slb_ref[...])
        vn_s[rs, :] = vn.astype(BF16)
    for gi in range(SGU_GROUPS):
        gs = slice(gi * SGU_GROUP_DIM, (gi + 1) * SGU_GROUP_DIM)
        rhs = jnp.concatenate(
            [vn_s[c * CHUNK:(c + 1) * CHUNK, gs] for c in range(n_chunks)], axis=1)
        mixed = _dot(wsp_ref[gi], rhs)
        for c in range(n_chunks):
            m = mixed[:, c * SGU_GROUP_DIM:(c + 1) * SGU_GROUP_DIM] + bsp_ref[:, gs]
            u = proj[c * CHUNK:(c + 1) * CHUNK, gs]
            act[c * CHUNK:(c + 1) * CHUNK, gs] = (u * m).astype(BF16)
    y_b = _dot(act[...], wso_ref[...])

    for rb in range(n_chunks):
        rs = slice(rb * CHUNK, (rb + 1) * CHUNK)
        ga = _sigmoid(proj[rs, 2 * d:3 * d] + gbias_ref[:, 0:d])
        gb = _sigmoid(proj[rs, 3 * d:4 * d] + gbias_ref[:, d:2 * d])
        act[rs, :] = (ga * y_a[rs, :] + gb * y_b[rs, :]).astype(BF16)
    o_ref[0] = x_ref[0] + _dot(act[...], wo_ref[...])


def _ffn_kernel(x_ref, nrm_ref, wg_ref, wu_ref, wd_ref, fin_ref, o_ref, hid, *, final_norm):
    x = x_ref[...]
    h = _rms(x, nrm_ref[...]).astype(BF16)
    d_ff = wg_ref.shape[1]
    for j in range(d_ff // FFN_COLS):
        cs = slice(j * FFN_COLS, (j + 1) * FFN_COLS)
        gt = _dot(h, wg_ref[:, cs])
        up = _dot(h, wu_ref[:, cs])
        hid[:, cs] = (_silu(gt) * up).astype(BF16)
    y = x + _dot(hid[...], wd_ref[...])
    if final_norm:
        y = _rms(y, fin_ref[...])
    o_ref[...] = y


def _resident(shape):
    nd = len(shape)
    return pl.BlockSpec(shape, lambda *_: (0,) * nd, pipeline_mode=pl.Buffered(1))


def _mixer(x, nrm, w_in, gbias, cw, cb, clg, clb, wco, slg, slb, wsp, bsp, wso, wo):
    bsz, seq, d = x.shape
    tile = MIX_TILE
    ext = tile + 2 * HALO
    assert seq % tile == 0 and tile % CHUNK == 0 and tile % HALO == 0
    assert ((tile + PACKED) // PACKED) % 8 != 0 and (ext - tile - PACKED) // 2 <= HALO
    assert CONV_PAD <= C_OFF and CONV_PAD <= HALO
    nt = seq // tile
    hb = tile // HALO
    last_hb = seq // HALO - 1
    n_slabs = d // LANES
    params = (nrm, w_in, gbias, cw, cb, clg, clb, wco, slg, slb, wsp, bsp, wso, wo)
    in_specs = [
        pl.BlockSpec((1, tile, d), lambda b, t: (b, t, 0)),
        pl.BlockSpec((1, HALO, d), lambda b, t: (b, jnp.maximum(t * hb - 1, 0), 0)),
        pl.BlockSpec((1, HALO, d), lambda b, t: (b, jnp.minimum((t + 1) * hb, last_hb), 0)),
    ] + [_resident(p.shape) for p in params]
    return pl.pallas_call(
        _mixer_kernel,
        grid=(bsz, nt),
        in_specs=in_specs,
        out_specs=pl.BlockSpec((1, tile, d), lambda b, t: (b, t, 0)),
        out_shape=jax.ShapeDtypeStruct(x.shape, x.dtype),
        scratch_shapes=[
            pltpu.VMEM((ext, d), BF16),
            pltpu.VMEM((n_slabs, ext + 2 * C_OFF, LANES), F32),
            pltpu.VMEM((n_slabs, ((tile + PACKED) // PACKED + 2 * CONV_PAD) * PACKED, LANES), BF16),
            pltpu.VMEM((tile, 4 * d), F32),
            pltpu.VMEM((n_slabs, ext, LANES), F32),
            pltpu.VMEM((tile, d), F32),
            pltpu.VMEM((tile, d), BF16),
            pltpu.VMEM((tile, d), BF16),
        ],
        compiler_params=pltpu.CompilerParams(
            dimension_semantics=("arbitrary", "arbitrary"),
            vmem_limit_bytes=V7X_VMEM_LIMIT_BYTES),
        name="mixer",
    )(x, x, x, *params)


def _ffn(x2, nrm, wg, wu, wd, fin, final_norm):
    n, d = x2.shape
    tile = FFN_TILE
    assert n % tile == 0 and wg.shape[1] % FFN_COLS == 0
    params = (nrm, wg, wu, wd, fin)
    return pl.pallas_call(
        functools.partial(_ffn_kernel, final_norm=final_norm),
        grid=(n // tile,),
        in_specs=[pl.BlockSpec((tile, d), lambda i: (i, 0))] + [_resident(p.shape) for p in params],
        out_specs=pl.BlockSpec((tile, d), lambda i: (i, 0)),
        out_shape=jax.ShapeDtypeStruct(x2.shape, x2.dtype),
        scratch_shapes=[pltpu.VMEM((tile, wg.shape[1]), BF16)],
        compiler_params=pltpu.CompilerParams(
            dimension_semantics=("arbitrary",),
            vmem_limit_bytes=V7X_VMEM_LIMIT_BYTES),
        name="ffn",
    )(x2, *params)


def _mixer_layer(x, l, norm_mix, w_in, gate_bias, conv_w, conv_b, conv_ln_g, conv_ln_b, w_conv_out,
                 sgu_ln_g, sgu_ln_b, w_spatial, b_spatial, w_sgu_out, w_o):
    d = x.shape[-1]
    n_slabs = d // LANES
    row = lambda v: v.reshape(1, -1).astype(F32)
    cw = conv_w[l].reshape(CONV_WIDTH, n_slabs, LANES).transpose(1, 0, 2).astype(BF16)
    cw = jnp.repeat(cw, PACKED, axis=1)
    cb = conv_b[l].reshape(n_slabs, 1, LANES).astype(F32)
    bsp = jnp.repeat(jnp.transpose(b_spatial[l]).astype(F32), SGU_GROUP_DIM, axis=1)
    return _mixer(
        x, row(norm_mix[l]), w_in[l].astype(BF16), row(gate_bias[l]), cw, cb,
        row(conv_ln_g[l]), row(conv_ln_b[l]), w_conv_out[l].astype(BF16),
        row(sgu_ln_g[l]), row(sgu_ln_b[l]), w_spatial[l].astype(BF16), bsp,
        w_sgu_out[l].astype(BF16), w_o[l].astype(BF16))


def kernel(x, norm_mix, w_in, gate_bias, conv_w, conv_b, conv_ln_g, conv_ln_b, w_conv_out, sgu_ln_g, sgu_ln_b, w_spatial, b_spatial, w_sgu_out, w_o, norm_ffn, w_ffn_gate, w_ffn_up, w_ffn_down, norm_final):
    bsz, seq, d = x.shape
    depth = w_in.shape[0]
    row = lambda v: v.reshape(1, -1).astype(F32)
    fin = row(norm_final)
    for l in range(depth):
        x = _mixer_layer(x, l, norm_mix, w_in, gate_bias, conv_w, conv_b, conv_ln_g, conv_ln_b,
                         w_conv_out, sgu_ln_g, sgu_ln_b, w_spatial, b_spatial, w_sgu_out, w_o)
        x = _ffn(
            x.reshape(bsz * seq, d), row(norm_ffn[l]), w_ffn_gate[l].astype(BF16),
            w_ffn_up[l].astype(BF16), w_ffn_down[l].astype(BF16), fin,
            final_norm=(l == depth - 1)).reshape(bsz, seq, d)
    return x
```
